```python
import jax
import jax.numpy as jnp
from jax import lax
import numpy as np

D_MODEL = 2048
BATCH = 2
SEQ = 4096
DEPTH = 4

N_MIXERS = 3
HEAD_DIM = 128
SB_HEADS = D_MODEL // HEAD_DIM
SB_DIM = SB_HEADS * HEAD_DIM
FOX_HEADS = D_MODEL // HEAD_DIM
FOX_DIM = FOX_HEADS * HEAD_DIM
GDN_K_HEADS = D_MODEL // HEAD_DIM
GDN_V_HEADS = 2 * GDN_K_HEADS
GDN_K_DIM = GDN_K_HEADS * HEAD_DIM
GDN_V_DIM = GDN_V_HEADS * HEAD_DIM
GDN_QKV_DIM = 2 * GDN_K_DIM + GDN_V_DIM
GDN_CONV = 4
GDN_CHUNK = 64
Q_BLOCK = 128
D_FF = 11 * D_MODEL // 4
N_EXPERTS = 8
TOP_K = 2
D_FF_EXPERT = D_FF // TOP_K
EPS = 1e-6

kernel_name = "hybrid_sb_gdn_fox_moe_trunk"


def rmsnorm(x, g):
    xf = x.astype(jnp.float32)
    y = xf * lax.rsqrt(jnp.mean(xf * xf, axis=-1, keepdims=True) + EPS)
    return (y * g.astype(jnp.float32)).astype(x.dtype)


def split_heads(t, n_heads):
    b, s, _ = t.shape
    return t.reshape(b, s, n_heads, -1).transpose(0, 2, 1, 3)


def merge_heads(t):
    b, h, s, d = t.shape
    return t.transpose(0, 2, 1, 3).reshape(b, s, h * d)


def l2norm(t):
    tf = t.astype(jnp.float32)
    return tf * lax.rsqrt(jnp.sum(tf * tf, axis=-1, keepdims=True) + EPS)


def swiglu(x, w_gate_up, w_down):
    gate, up = jnp.split(x @ w_gate_up, 2, axis=-1)
    return (jax.nn.silu(gate) * up) @ w_down


def stick_breaking_attention(q, k, v):
    seq, dh = q.shape[2], q.shape[3]
    scale = dh ** -0.5
    outs = []
    for start in range(0, seq, Q_BLOCK):
        end = start + Q_BLOCK
        z = jnp.einsum('bhqd,bhkd->bhqk', q[:, :, start:end], k[:, :, :end],
                       preferred_element_type=jnp.float32) * scale
        t_idx = start + jnp.arange(Q_BLOCK)[:, None]
        s_idx = jnp.arange(end)[None, :]
        strict = s_idx < t_idx
        log_beta = jax.nn.log_sigmoid(z)
        log_keep = jnp.where(strict, jax.nn.log_sigmoid(-z), 0.0)
        later = lax.cumsum(log_keep, axis=3, reverse=True) - log_keep
        w = jnp.where(strict, jnp.exp(log_beta + later), 0.0)
        outs.append(jnp.einsum('bhqk,bhkd->bhqd', w.astype(v.dtype), v[:, :, :end]))
    return jnp.concatenate(outs, axis=2)


def sb_mixer(x, w_in, w_out):
    q, k, v = jnp.split(x @ w_in, 3, axis=-1)
    o = stick_breaking_attention(split_heads(q, SB_HEADS), split_heads(k, SB_HEADS),
                                 split_heads(v, SB_HEADS))
    return merge_heads(o) @ w_out


def causal_depthwise_conv(x, w):
    c, width = w.shape
    rhs = w.T[:, None, :].astype(x.dtype)
    return lax.conv_general_dilated(x, rhs, window_strides=(1,), padding=[(width - 1, 0)],
                                    dimension_numbers=('NWC', 'WIO', 'NWC'),
                                    feature_group_count=c)


def gated_delta_rule(q, k, v, g, beta):
    b, h, seq, dk = q.shape
    dv = v.shape[-1]
    c = GDN_CHUNK
    n = seq // c
    f32 = jnp.float32
    q = q.astype(f32).reshape(b, h, n, c, dk)
    k = k.astype(f32).reshape(b, h, n, c, dk)
    v = v.astype(f32).reshape(b, h, n, c, dv)
    beta = beta.astype(f32).reshape(b, h, n, c)
    g = lax.cumsum(g.astype(f32).reshape(b, h, n, c), axis=3)
    k_beta = k * beta[..., None]
    v_beta = v * beta[..., None]
    incl = jnp.tril(jnp.ones((c, c), dtype=bool))
    strict = jnp.tril(jnp.ones((c, c), dtype=bool), -1)
    decay = jnp.exp(jnp.where(incl, g[..., :, None] - g[..., None, :], -jnp.inf))
    m = jnp.where(strict, jnp.einsum('bhnid,bhnjd->bhnij', k_beta, k) * decay, 0.0)
    t_mat = m + jnp.eye(c, dtype=f32)
    u = lax.linalg.triangular_solve(t_mat, v_beta, left_side=True, lower=True,
                                    unit_diagonal=True)
    w = lax.linalg.triangular_solve(t_mat, k_beta * jnp.exp(g)[..., None], left_side=True,
                                    lower=True, unit_diagonal=True)
    qk = jnp.where(incl, jnp.einsum('bhnid,bhnjd->bhnij', q, k) * decay, 0.0)

    def step(state, inp):
        q_c, k_c, u_c, w_c, g_c, qk_c = inp
        v_new = u_c - jnp.einsum('bhik,bhkv->bhiv', w_c, state)
        o = (jnp.einsum('bhik,bhkv->bhiv', q_c * jnp.exp(g_c)[..., None], state)
             + jnp.einsum('bhij,bhjv->bhiv', qk_c, v_new))
        g_last = g_c[..., -1]
        k_dec = k_c * jnp.exp(g_last[..., None] - g_c)[..., None]
        state = state * jnp.exp(g_last)[..., None, None] + jnp.einsum('bhik,bhiv->bhkv', k_dec, v_new)
        return state, o

    xs = tuple(jnp.moveaxis(a, 2, 0) for a in (q, k, u, w, g, qk))
    _, o = lax.scan(step, jnp.zeros((b, h, dk, dv), f32), xs)
    return jnp.moveaxis(o, 0, 2).reshape(b, h, seq, dv)


def gdn_mixer(x, w_in, conv_w, a_log, dt_bias, norm_w, w_out):
    proj = x @ w_in
    qkv, z, b_in, a_in = jnp.split(
        proj, [GDN_QKV_DIM, GDN_QKV_DIM + GDN_V_DIM, GDN_QKV_DIM + GDN_V_DIM + GDN_V_HEADS], axis=-1)
    qkv = jax.nn.silu(causal_depthwise_conv(qkv, conv_w))
    q, k, v = jnp.split(qkv, [GDN_K_DIM, 2 * GDN_K_DIM], axis=-1)
    rep = GDN_V_HEADS // GDN_K_HEADS
    q = jnp.repeat(l2norm(split_heads(q, GDN_K_HEADS)) * (HEAD_DIM ** -0.5), rep, axis=1)
    k = jnp.repeat(l2norm(split_heads(k, GDN_K_HEADS)), rep, axis=1)
    v = split_heads(v, GDN_V_HEADS)
    beta = jax.nn.sigmoid(b_in.astype(jnp.float32)).transpose(0, 2, 1)
    g = (-jnp.exp(a_log.astype(jnp.float32))
         * jax.nn.softplus(a_in.astype(jnp.float32) + dt_bias.astype(jnp.float32))).transpose(0, 2, 1)
    o = gated_delta_rule(q, k, v, g, beta)
    zf = split_heads(z, GDN_V_HEADS).astype(jnp.float32)
    o = (o * lax.rsqrt(jnp.mean(o * o, axis=-1, keepdims=True) + EPS)
         * norm_w.astype(jnp.float32) * jax.nn.silu(zf))
    return merge_heads(o.astype(x.dtype)) @ w_out


def forgetting_attention(q, k, v, log_f):
    seq, dh = q.shape[2], q.shape[3]
    scale = dh ** -0.5
    cum = lax.cumsum(log_f.astype(jnp.float32), axis=2)
    outs = []
    for start in range(0, seq, Q_BLOCK):
        end = start + Q_BLOCK
        logits = (jnp.einsum('bhqd,bhkd->bhqk', q[:, :, start:end], k[:, :, :end],
                             preferred_element_type=jnp.float32) * scale
                  + cum[:, :, start:end, None] - cum[:, :, None, :end])
        causal = jnp.arange(end)[None, :] <= (start + jnp.arange(Q_BLOCK)[:, None])
        p = jax.nn.softmax(jnp.where(causal, logits, -jnp.inf), axis=-1)
        outs.append(jnp.einsum('bhqk,bhkd->bhqd', p.astype(v.dtype), v[:, :, :end]))
    return jnp.concatenate(outs, axis=2)


def fox_mixer(x, w_in, b_f, w_out):
    q, k, v, gate, f = jnp.split(x @ w_in, [FOX_DIM, 2 * FOX_DIM, 3 * FOX_DIM, 4 * FOX_DIM], axis=-1)
    log_f = jax.nn.log_sigmoid(f.astype(jnp.float32) + b_f.astype(jnp.float32)).transpose(0, 2, 1)
    o = forgetting_attention(split_heads(q, FOX_HEADS), split_heads(k, FOX_HEADS),
                             split_heads(v, FOX_HEADS), log_f)
    return (merge_heads(o) * jax.nn.sigmoid(gate)) @ w_out


def moe_swiglu(x, w_router, w_gate_up, w_down):
    logits = jnp.einsum('bsd,de->bse', x, w_router).astype(jnp.float32)
    top_logits, top_idx = lax.top_k(logits, TOP_K)
    top_w = jax.nn.softmax(top_logits, axis=-1)
    combine = jnp.einsum('bske,bsk->bse', jax.nn.one_hot(top_idx, N_EXPERTS, dtype=jnp.float32),
                         top_w).astype(x.dtype)
    out = jnp.zeros_like(x)
    for e in range(N_EXPERTS):
        out = out + combine[..., e:e + 1] * swiglu(x, w_gate_up[e], w_down[e])
    return out


def setup_inputs(seed: int = 0) -> dict:
    key = jax.random.key(seed)
    keys = iter(jax.random.split(key, 32))
    f32 = jnp.float32

    def normal(shape, scale):
        return jax.random.normal(next(keys), shape, f32) * scale

    def gain(shape):
        return 1.0 + 0.02 * jax.random.normal(next(keys), shape, f32)

    n_sb = (DEPTH + 2) // 3
    n_gdn = (DEPTH + 1) // 3
    n_fox = DEPTH // 3
    n_dense = (DEPTH + 1) // 2
    n_moe = DEPTH // 2
    d = D_MODEL
    gdn_in_cols = GDN_QKV_DIM + GDN_V_DIM + 2 * GDN_V_HEADS
    fox_in_cols = 4 * FOX_DIM + FOX_HEADS
    dt = jnp.exp(jax.random.uniform(next(keys), (n_gdn, GDN_V_HEADS), f32,
                                    minval=np.log(1e-3), maxval=np.log(1e-1)))
    return {
        "x": normal((BATCH, SEQ, d), 1.0),
        "norm_mix": gain((DEPTH, d)),
        "norm_ffn": gain((DEPTH, d)),
        "sb_w_in": normal((n_sb, d, 3 * SB_DIM), d ** -0.5),
        "sb_w_out": normal((n_sb, SB_DIM, d), SB_DIM ** -0.5),
        "gdn_w_in": normal((n_gdn, d, gdn_in_cols), d ** -0.5),
        "gdn_conv_w": normal((n_gdn, GDN_QKV_DIM, GDN_CONV), GDN_CONV ** -0.5),
        "gdn_a_log": jnp.log(jax.random.uniform(next(keys), (n_gdn, GDN_V_HEADS), f32,
                                                minval=1.0, maxval=16.0)),
        "gdn_dt_bias": dt + jnp.log(-jnp.expm1(-dt)),
        "gdn_norm_w": gain((n_gdn, HEAD_DIM)),
        "gdn_w_out": normal((n_gdn, GDN_V_DIM, d), GDN_V_DIM ** -0.5),
        "fox_w_in": normal((n_fox, d, fox_in_cols), d ** -0.5),
        "fox_b_f": jax.random.uniform(next(keys), (n_fox, FOX_HEADS), f32, minval=1.0, maxval=4.0),
        "fox_w_out": normal((n_fox, FOX_DIM, d), FOX_DIM ** -0.5),
        "ffn_w_gate_up": normal((n_dense, d, 2 * D_FF), d ** -0.5),
        "ffn_w_down": normal((n_dense, D_FF, d), D_FF ** -0.5),
        "moe_w_router": normal((n_moe, d, N_EXPERTS), d ** -0.5),
        "moe_w_gate_up": normal((n_moe, N_EXPERTS, d, 2 * D_FF_EXPERT), d ** -0.5),
        "moe_w_down": normal((n_moe, N_EXPERTS, D_FF_EXPERT, d), D_FF_EXPERT ** -0.5),
        "final_norm": gain((d,)),
    }


def reference(x, norm_mix, norm_ffn, sb_w_in, sb_w_out, gdn_w_in, gdn_conv_w, gdn_a_log,
              gdn_dt_bias, gdn_norm_w, gdn_w_out, fox_w_in, fox_b_f, fox_w_out,
              ffn_w_gate_up, ffn_w_down, moe_w_router, moe_w_gate_up, moe_w_down, final_norm):
    h = x
    for i in range(DEPTH):
        kind = i % N_MIXERS
        j = i // N_MIXERS
        hn = rmsnorm(h, norm_mix[i])
        if kind == 0:
            mix = sb_mixer(hn, sb_w_in[j], sb_w_out[j])
        elif kind == 1:
            mix = gdn_mixer(hn, gdn_w_in[j], gdn_conv_w[j], gdn_a_log[j], gdn_dt_bias[j],
                            gdn_norm_w[j], gdn_w_out[j])
        else:
            mix = fox_mixer(hn, fox_w_in[j], fox_b_f[j], fox_w_out[j])
        h = h + mix
        hn = rmsnorm(h, norm_ffn[i])
        f = i // 2
        if i % 2 == 0:
            h = h + swiglu(hn, ffn_w_gate_up[f], ffn_w_down[f])
        else:
            h = h + moe_swiglu(hn, moe_w_router[f], moe_w_gate_up[f], moe_w_down[f])
    return rmsnorm(h, final_norm)
```

```python
import functools

import jax
import jax.numpy as jnp
from jax import lax
from jax.experimental import pallas as pl
from jax.experimental.pallas import tpu as pltpu

HEAD_DIM = 128
LANES = 128
GDN_CONV = 4
GDN_CHUNK = 128
N_EXPERTS = 8
EPS = 1e-6
VMEM_LIMIT_BYTES = 56 * 1024 * 1024

F32 = jnp.float32
BF16 = jnp.bfloat16


def _params(*semantics):
    return pltpu.CompilerParams(dimension_semantics=semantics, vmem_limit_bytes=VMEM_LIMIT_BYTES)


def _tile(dim, preferred):
    assert dim % LANES == 0, dim
    t = min(preferred, dim) // LANES * LANES
    while dim % t:
        t -= LANES
    return t


def _dot(a, b):
    return jnp.dot(a, b, preferred_element_type=F32)


def _dot_nt(a, b):
    return lax.dot_general(a, b, (((1,), (1,)), ((), ())), preferred_element_type=F32)


def _dot_tn(a, b):
    return lax.dot_general(a, b, (((0,), (0,)), ((), ())), preferred_element_type=F32)


def _split3(x):
    hi = x.astype(BF16)
    r = x - hi.astype(F32)
    mid = r.astype(BF16)
    lo = (r - mid.astype(F32)).astype(BF16)
    return hi, mid, lo


def _split2(x):
    hi = x.astype(BF16)
    lo = (x - hi.astype(F32)).astype(BF16)
    return hi, lo


def _softplus(z):
    return jnp.maximum(z, 0.0) + jnp.log(1.0 + jnp.exp(-jnp.abs(z)))


def _sigmoid(z):
    return 1.0 / (1.0 + jnp.exp(-z))


def _silu(z):
    return z * _sigmoid(z)


def _block_id(idx, size):
    shift = size.bit_length() - 1
    assert 1 << shift == size
    return jnp.right_shift(idx, shift)


def _rmsnorm_kernel(x_ref, g_ref, o_ref):
    x = x_ref[...]
    ms = jnp.mean(x * x, axis=-1, keepdims=True)
    o_ref[...] = (x * lax.rsqrt(ms + EPS) * g_ref[...]).astype(o_ref.dtype)


def rmsnorm(x, g, out_dtype, tm=512):
    m, d = x.shape
    return pl.pallas_call(
        _rmsnorm_kernel,
        grid=(m // tm,),
        in_specs=[pl.BlockSpec((tm, d), lambda i: (i, 0)),
                  pl.BlockSpec((1, d), lambda i: (0, 0))],
        out_specs=pl.BlockSpec((tm, d), lambda i: (i, 0)),
        out_shape=jax.ShapeDtypeStruct((m, d), out_dtype),
        compiler_params=_params("parallel"),
        name="rmsnorm",
    )(x, g.reshape(1, d))


def _matmul_kernel(*refs, nk, has_res, has_scale):
    a_ref, w_ref = refs[0], refs[1]
    pos = 2
    res_ref = scale_ref = None
    if has_res:
        res_ref = refs[pos]
        pos += 1
    if has_scale:
        scale_ref = refs[pos]
        pos += 1
    o_ref, acc_ref = refs[pos], refs[pos + 1]
    k = pl.program_id(2)

    @pl.when(k == 0)
    def _():
        acc_ref[...] = jnp.zeros_like(acc_ref)

    acc_ref[...] += _dot(a_ref[...], w_ref[...])

    @pl.when(k == nk - 1)
    def _():
        r = acc_ref[...]
        if has_scale:
            r = r * scale_ref[...]
        if has_res:
            r = r + res_ref[...]
        o_ref[...] = r.astype(o_ref.dtype)


def matmul(a, w, out_dtype, res=None, scale=None, tm=1024, tn=1024, tk=512):
    m, kdim = a.shape
    n = w.shape[1]
    tm, tn, tk = _tile(m, tm), _tile(n, tn), _tile(kdim, tk)
    nk = kdim // tk
    in_specs = [pl.BlockSpec((tm, tk), lambda i, j, k: (i, k)),
                pl.BlockSpec((tk, tn), lambda i, j, k: (k, j))]
    args = [a, w]
    if res is not None:
        in_specs.append(pl.BlockSpec((tm, tn), lambda i, j, k: (i, j)))
        args.append(res)
    if scale is not None:
        in_specs.append(pl.BlockSpec((tm, 1), lambda i, j, k: (i, 0)))
        args.append(scale)
    return pl.pallas_call(
        functools.partial(_matmul_kernel, nk=nk, has_res=res is not None, has_scale=scale is not None),
        grid=(m // tm, n // tn, nk),
        in_specs=in_specs,
        out_specs=pl.BlockSpec((tm, tn), lambda i, j, k: (i, j)),
        out_shape=jax.ShapeDtypeStruct((m, n), out_dtype),
        scratch_shapes=[pltpu.VMEM((tm, tn), F32)],
        compiler_params=_params("parallel", "parallel", "arbitrary"),
        name="matmul",
    )(*args)


def _swiglu_up_kernel(a_ref, wg_ref, wu_ref, o_ref, accg_ref, accu_ref, *, nk):
    k = pl.program_id(2)

    @pl.when(k == 0)
    def _():
        accg_ref[...] = jnp.zeros_like(accg_ref)
        accu_ref[...] = jnp.zeros_like(accu_ref)

    a = a_ref[...]
    accg_ref[...] += _dot(a, wg_ref[...])
    accu_ref[...] += _dot(a, wu_ref[...])

    @pl.when(k == nk - 1)
    def _():
        o_ref[...] = (_silu(accg_ref[...]) * accu_ref[...]).astype(o_ref.dtype)


def swiglu_up(a, w_gate_up, tm=1024, tn=512, tk=512):
    m, kdim = a.shape
    f = w_gate_up.shape[1] // 2
    tm, tn, tk = _tile(m, tm), _tile(f, tn), _tile(kdim, tk)
    nk, nj = kdim // tk, f // tn
    return pl.pallas_call(
        functools.partial(_swiglu_up_kernel, nk=nk),
        grid=(m // tm, nj, nk),
        in_specs=[pl.BlockSpec((tm, tk), lambda i, j, k: (i, k)),
                  pl.BlockSpec((tk, tn), lambda i, j, k: (k, j)),
                  pl.BlockSpec((tk, tn), lambda i, j, k: (k, j + nj))],
        out_specs=pl.BlockSpec((tm, tn), lambda i, j, k: (i, j)),
        out_shape=jax.ShapeDtypeStruct((m, f), BF16),
        scratch_shapes=[pltpu.VMEM((tm, tn), F32), pltpu.VMEM((tm, tn), F32)],
        compiler_params=_params("parallel", "parallel", "arbitrary"),
        name="swiglu_up",
    )(a, w_gate_up, w_gate_up)


def _small_proj_kernel(h_ref, g_ref, w_ref, o_ref):
    x = h_ref[...]
    ms = jnp.mean(x * x, axis=-1, keepdims=True)
    xn = x * lax.rsqrt(ms + EPS) * g_ref[...]
    xh, xl = _split2(xn)
    wh, wl = _split2(w_ref[...])
    o_ref[...] = _dot(xh, wh) + _dot(xl, wh) + _dot(xh, wl)


def small_proj(h, g, w, tm=512):
    m, d = h.shape
    n = w.shape[1]
    return pl.pallas_call(
        _small_proj_kernel,
        grid=(m // tm,),
        in_specs=[pl.BlockSpec((tm, d), lambda i: (i, 0)),
                  pl.BlockSpec((1, d), lambda i: (0, 0)),
                  pl.BlockSpec((d, n), lambda i: (0, 0))],
        out_specs=pl.BlockSpec((tm, n), lambda i: (i, 0)),
        out_shape=jax.ShapeDtypeStruct((m, n), F32),
        compiler_params=_params("parallel"),
        name="small_proj",
    )(h, g.reshape(1, d), w)


def _pad_cols(w, width):
    return jnp.pad(w, ((0, 0), (0, width - w.shape[1])))


def _sb_attn_kernel(q_ref, k_ref, v_ref, o_ref, *, t, scale):
    i = pl.program_id(2)
    q = q_ref[0]
    row = lax.broadcasted_iota(jnp.int32, (t, t), 0)
    col = lax.broadcasted_iota(jnp.int32, (t, t), 1)
    strict = col < row
    suffix = jnp.where(row > col, 1.0, 0.0).astype(BF16)

    def block(kb, carry, acc, masked):
        start = pl.multiple_of(kb * t, t)
        k = k_ref[0, pl.ds(start, t), :]
        v = v_ref[0, pl.ds(start, t), :]
        z = _dot_nt(q, k) * scale
        sp = _softplus(z)
        log_keep = -sp
        if masked:
            log_keep = jnp.where(strict, log_keep, 0.0)
        hi, lo = _split2(log_keep)
        later = _dot(hi, suffix) + _dot(lo, suffix) + carry
        w = jnp.exp(z - sp + later)
        if masked:
            w = jnp.where(strict, w, 0.0)
        acc = acc + _dot(w.astype(BF16), v)
        carry = carry + jnp.sum(log_keep, axis=1, keepdims=True)
        return carry, acc

    carry, acc = block(i, jnp.zeros((t, 1), F32), jnp.zeros((t, HEAD_DIM), F32), True)

    def body(n, c):
        return block(i - 1 - n, c[0], c[1], False)

    carry, acc = lax.fori_loop(0, i, body, (carry, acc))
    o_ref[0] = acc.astype(o_ref.dtype)


def sb_attention(qkv, n_heads, t=256):
    b, s, _ = qkv.shape
    h = n_heads
    return pl.pallas_call(
        functools.partial(_sb_attn_kernel, t=t, scale=HEAD_DIM ** -0.5),
        grid=(b, h, s // t),
        in_specs=[pl.BlockSpec((1, t, HEAD_DIM), lambda bi, hi, i: (bi, i, hi)),
                  pl.BlockSpec((1, s, HEAD_DIM), lambda bi, hi, i: (bi, 0, h + hi)),
                  pl.BlockSpec((1, s, HEAD_DIM), lambda bi, hi, i: (bi, 0, 2 * h + hi))],
        out_specs=pl.BlockSpec((1, t, HEAD_DIM), lambda bi, hi, i: (bi, i, hi)),
        out_shape=jax.ShapeDtypeStruct((b, s, h * HEAD_DIM), BF16),
        compiler_params=_params("parallel", "parallel", "arbitrary"),
        name="sb_attention",
    )(qkv, qkv, qkv)


def _fox_gate_kernel(f_ref, bias_ref, col_ref, row_ref, carry_ref, *, t):
    i = pl.program_id(1)

    @pl.when(i == 0)
    def _():
        carry_ref[...] = jnp.zeros_like(carry_ref)

    x = f_ref[0] + bias_ref[...]
    log_f = -_softplus(-x)
    row = lax.broadcasted_iota(jnp.int32, (t, t), 0)
    col = lax.broadcasted_iota(jnp.int32, (t, t), 1)
    lower = jnp.where(col <= row, 1.0, 0.0).astype(BF16)
    hi, mid, lo = _split3(log_f)
    cum = _dot(lower, hi) + _dot(lower, mid) + _dot(lower, lo) + carry_ref[...]
    col_ref[0] = cum
    row_ref[0] = cum.T
    carry_ref[...] = cum[t - 1:t, :]


def fox_gate(f_raw, bias, t=512):
    b, s, n = f_raw.shape
    return pl.pallas_call(
        functools.partial(_fox_gate_kernel, t=t),
        grid=(b, s // t),
        in_specs=[pl.BlockSpec((1, t, n), lambda bi, i: (bi, i, 0)),
                  pl.BlockSpec((1, n), lambda bi, i: (0, 0))],
        out_specs=[pl.BlockSpec((1, t, n), lambda bi, i: (bi, i, 0)),
                   pl.BlockSpec((1, n, t), lambda bi, i: (bi, 0, i))],
        out_shape=[jax.ShapeDtypeStruct((b, s, n), F32), jax.ShapeDtypeStruct((b, n, s), F32)],
        scratch_shapes=[pltpu.VMEM((1, n), F32)],
        compiler_params=_params("parallel", "arbitrary"),
        name="fox_gate",
    )(f_raw, bias)


def _fox_attn_kernel(q_ref, k_ref, v_ref, gate_ref, cq_ref, ck_ref, o_ref, *, t, scale):
    hi = pl.program_id(1)
    i = pl.program_id(2)
    q = q_ref[0]
    lane = lax.broadcasted_iota(jnp.int32, (t, LANES), 1)
    cum_q = jnp.sum(jnp.where(lane == hi, cq_ref[0], 0.0), axis=1, keepdims=True)
    row = lax.broadcasted_iota(jnp.int32, (t, t), 0)
    col = lax.broadcasted_iota(jnp.int32, (t, t), 1)
    causal = col <= row

    def block(kb, m, l, acc, masked):
        start = pl.multiple_of(kb * t, t)
        k = k_ref[0, pl.ds(start, t), :]
        v = v_ref[0, pl.ds(start, t), :]
        cum_k = ck_ref[0, 0, pl.ds(kb, 1), :]
        logits = _dot_nt(q, k) * scale + cum_q - cum_k
        if masked:
            logits = jnp.where(causal, logits, -jnp.inf)
        m_new = jnp.maximum(m, jnp.max(logits, axis=1, keepdims=True))
        alpha = jnp.exp(m - m_new)
        p = jnp.exp(logits - m_new)
        l = l * alpha + jnp.sum(p, axis=1, keepdims=True)
        acc = acc * alpha + _dot(p.astype(BF16), v)
        return m_new, l, acc

    init = (jnp.full((t, 1), -jnp.inf, F32), jnp.zeros((t, 1), F32), jnp.zeros((t, HEAD_DIM), F32))
    m, l, acc = block(i, *init, True)

    def body(n, c):
        return block(i - 1 - n, *c, False)

    m, l, acc = lax.fori_loop(0, i, body, (m, l, acc))
    out = acc / l * _sigmoid(gate_ref[0].astype(F32))
    o_ref[0] = out.astype(o_ref.dtype)


def fox_attention(proj, cum_col, cum_row, n_heads, t=256):
    b, s, _ = proj.shape
    h = n_heads
    cum_row = cum_row.reshape(b, h, s // t, t)
    return pl.pallas_call(
        functools.partial(_fox_attn_kernel, t=t, scale=HEAD_DIM ** -0.5),
        grid=(b, h, s // t),
        in_specs=[pl.BlockSpec((1, t, HEAD_DIM), lambda bi, hi, i: (bi, i, hi)),
                  pl.BlockSpec((1, s, HEAD_DIM), lambda bi, hi, i: (bi, 0, h + hi)),
                  pl.BlockSpec((1, s, HEAD_DIM), lambda bi, hi, i: (bi, 0, 2 * h + hi)),
                  pl.BlockSpec((1, t, HEAD_DIM), lambda bi, hi, i: (bi, i, 3 * h + hi)),
                  pl.BlockSpec((1, t, LANES), lambda bi, hi, i: (bi, i, 0)),
                  pl.BlockSpec((1, 1, s // t, t), lambda bi, hi, i: (bi, hi, 0, 0))],
        out_specs=pl.BlockSpec((1, t, HEAD_DIM), lambda bi, hi, i: (bi, i, hi)),
        out_shape=jax.ShapeDtypeStruct((b, s, h * HEAD_DIM), BF16),
        compiler_params=_params("parallel", "parallel", "arbitrary"),
        name="fox_attention",
    )(proj, proj, proj, proj, cum_col, cum_row)


def _gdn_gate_kernel(b_ref, a_ref, alog_ref, dt_ref, beta_ref, gc_ref, gl_ref, gct_ref, *, t, chunk):
    beta_ref[0] = _sigmoid(b_ref[0])
    g = -jnp.exp(alog_ref[...]) * _softplus(a_ref[0] + dt_ref[...])
    row = lax.broadcasted_iota(jnp.int32, (t, t), 0)
    col = lax.broadcasted_iota(jnp.int32, (t, t), 1)
    same = _block_id(row, chunk) == _block_id(col, chunk)
    lower = jnp.where(same & (col <= row), 1.0, 0.0).astype(BF16)
    whole = jnp.where(same, 1.0, 0.0).astype(BF16)
    hi, mid, lo = _split3(g)
    gc = _dot(lower, hi) + _dot(lower, mid) + _dot(lower, lo)
    gc_ref[0] = gc
    gl_ref[0] = _dot(whole, hi) + _dot(whole, mid) + _dot(whole, lo)
    gct_ref[0] = gc.T


def gdn_gates(b_raw, a_raw, a_log, dt_bias, t=512):
    b, s, n = b_raw.shape
    spec = pl.BlockSpec((1, t, n), lambda bi, i: (bi, i, 0))
    vec = pl.BlockSpec((1, n), lambda bi, i: (0, 0))
    return pl.pallas_call(
        functools.partial(_gdn_gate_kernel, t=t, chunk=GDN_CHUNK),
        grid=(b, s // t),
        in_specs=[spec, spec, vec, vec],
        out_specs=[spec, spec, spec, pl.BlockSpec((1, n, t), lambda bi, i: (bi, 0, i))],
        out_shape=[jax.ShapeDtypeStruct((b, s, n), F32)] * 3 + [jax.ShapeDtypeStruct((b, n, s), F32)],
        compiler_params=_params("parallel", "parallel"),
        name="gdn_gates",
    )(b_raw, a_raw, a_log, dt_bias)


def _gdn_prep_kernel(x_ref, prev_ref, w_ref, o_ref, *, ts, n_q_blocks, n_qk_blocks, heads_per_block):
    si = pl.program_id(1)
    ci = pl.program_id(2)
    cur = x_ref[0].astype(F32)
    prev = jnp.where(si == 0, 0.0, prev_ref[0].astype(F32))
    ext = jnp.concatenate([prev, cur], axis=0)
    w = w_ref[...]
    y = jnp.zeros_like(cur)
    for tap in range(GDN_CONV):
        off = 8 - (GDN_CONV - 1) + tap
        y = y + w[tap:tap + 1, :] * ext[off:off + ts, :]
    y = _silu(y)
    q_scale = jnp.where(ci < n_q_blocks, HEAD_DIM ** -0.5, 1.0)
    is_qk = ci < n_qk_blocks
    for hh in range(heads_per_block):
        sl = slice(hh * HEAD_DIM, (hh + 1) * HEAD_DIM)
        yh = y[:, sl]
        ss = jnp.sum(yh * yh, axis=1, keepdims=True)
        factor = jnp.where(is_qk, lax.rsqrt(ss + EPS) * q_scale, 1.0)
        o_ref[0, :, sl] = (yh * factor).astype(o_ref.dtype)


def gdn_prep(proj, conv_w_t, qkv_dim, k_dim, ts=512, tc=512):
    b, s, _ = proj.shape
    ts, tc = min(ts, s), min(tc, k_dim)
    rows8 = ts // 8
    return pl.pallas_call(
        functools.partial(_gdn_prep_kernel, ts=ts, n_q_blocks=k_dim // tc, n_qk_blocks=2 * k_dim // tc,
                          heads_per_block=tc // HEAD_DIM),
        grid=(b, s // ts, qkv_dim // tc),
        in_specs=[pl.BlockSpec((1, ts, tc), lambda bi, si, ci: (bi, si, ci)),
                  pl.BlockSpec((1, 8, tc), lambda bi, si, ci: (bi, jnp.maximum(si * rows8 - 1, 0), ci)),
                  pl.BlockSpec((GDN_CONV, tc), lambda bi, si, ci: (0, ci))],
        out_specs=pl.BlockSpec((1, ts, tc), lambda bi, si, ci: (bi, si, ci)),
        out_shape=jax.ShapeDtypeStruct((b, s, qkv_dim), BF16),
        compiler_params=_params("parallel", "parallel", "parallel"),
        name="gdn_prep",
    )(proj, proj, conv_w_t)


def _inv_unit_lower(m_strict, row, col):
    c = m_strict.shape[0]
    eye = jnp.where(row == col, 1.0, 0.0)
    n1 = jnp.where(_block_id(row, 8) == _block_id(col, 8), -m_strict, 0.0)
    n1b = n1.astype(BF16)
    n2 = _dot(n1b, n1b)
    n2b = n2.astype(BF16)
    n4 = _dot(n2b, n2b)
    x = _dot((eye + n1).astype(BF16), (eye + n2).astype(BF16))
    x = _dot(x.astype(BF16), (eye + n4).astype(BF16))
    size = 8
    while size < c:
        lower_left = ((_block_id(row, size) == _block_id(col, size) + 1)
                      & (_block_id(row, 2 * size) == _block_id(col, 2 * size)))
        cm = jnp.where(lower_left, m_strict, 0.0).astype(BF16)
        xb = x.astype(BF16)
        x = x - _dot(_dot(xb, cm).astype(BF16), xb)
        size *= 2
    return x


def _gdn_kernel(q_ref, k_ref, v_ref, z_ref, beta_ref, gc_ref, gl_ref, gct_ref, nw_ref, o_ref, state_ref,
                *, n_chunks):
    ki = pl.program_id(1)
    si = pl.program_id(2)
    c = GDN_CHUNK
    t = n_chunks * c

    @pl.when(si == 0)
    def _():
        state_ref[...] = jnp.zeros_like(state_ref)

    lane = lax.broadcasted_iota(jnp.int32, (t, LANES), 1)
    row = lax.broadcasted_iota(jnp.int32, (c, c), 0)
    col = lax.broadcasted_iota(jnp.int32, (c, c), 1)
    strict = col < row
    incl = col <= row

    def column(ref, head):
        return jnp.sum(jnp.where(lane == head, ref[0], 0.0), axis=1, keepdims=True)

    cols = [(column(beta_ref, 2 * ki + hh), column(gc_ref, 2 * ki + hh), column(gl_ref, 2 * ki + hh))
            for hh in range(2)]
    nw = nw_ref[...]

    local = []
    for ch in range(n_chunks):
        rows = slice(ch * c, (ch + 1) * c)
        qc = q_ref[0, rows, :]
        kc = k_ref[0, rows, :]
        qf = qc.astype(F32)
        kf = kc.astype(F32)
        kk = _dot_nt(kc, kc)
        qk = _dot_nt(qc, kc)
        for hh in range(2):
            beta, gcum, glast = (x[rows] for x in cols[hh])
            g_row = gct_ref[0, 0, hh:hh + 1, rows]
            decay = jnp.exp(jnp.minimum(gcum - g_row, 0.0))
            m_strict = jnp.where(strict, kk * beta * decay, 0.0)
            attn = jnp.where(incl, qk * decay, 0.0)
            t_inv = _inv_unit_lower(m_strict, row, col).astype(BF16)
            hs = slice(hh * HEAD_DIM, (hh + 1) * HEAD_DIM)
            v_beta = v_ref[0, rows, hs].astype(F32) * beta
            k_beta = kf * (beta * jnp.exp(gcum))
            u = _dot(t_inv, v_beta.astype(BF16))
            w = _dot(t_inv, k_beta.astype(BF16))
            q_dec = (qf * jnp.exp(gcum)).astype(BF16)
            k_dec = (kf * jnp.exp(glast - gcum)).astype(BF16)
            local.append((u, w.astype(BF16), attn.astype(BF16), q_dec, k_dec, jnp.exp(glast[0:1, :])))

    for ch in range(n_chunks):
        rows = slice(ch * c, (ch + 1) * c)
        for hh in range(2):
            u, w, attn, q_dec, k_dec, chunk_decay = local[ch * 2 + hh]
            hs = slice(hh * HEAD_DIM, (hh + 1) * HEAD_DIM)
            state = state_ref[hh]
            state_b = state.astype(BF16)
            v_new = u - _dot(w, state_b)
            v_new_b = v_new.astype(BF16)
            o = _dot(q_dec, state_b) + _dot(attn, v_new_b)
            state_ref[hh] = state * chunk_decay + _dot_tn(k_dec, v_new_b)
            ms = jnp.mean(o * o, axis=1, keepdims=True)
            zf = z_ref[0, rows, hs].astype(F32)
            o_ref[0, rows, hs] = (o * lax.rsqrt(ms + EPS) * nw * _silu(zf)).astype(o_ref.dtype)


def gdn_core(qkv_act, proj, beta, gc, gl, gct, norm_w, n_k_heads, n_chunks=2):
    b, s, _ = qkv_act.shape
    hk = n_k_heads
    t = n_chunks * GDN_CHUNK
    gate = pl.BlockSpec((1, t, LANES), lambda bi, ki, si: (bi, si, 0))
    return pl.pallas_call(
        functools.partial(_gdn_kernel, n_chunks=n_chunks),
        grid=(b, hk, s // t),
        in_specs=[pl.BlockSpec((1, t, HEAD_DIM), lambda bi, ki, si: (bi, si, ki)),
                  pl.BlockSpec((1, t, HEAD_DIM), lambda bi, ki, si: (bi, si, hk + ki)),
                  pl.BlockSpec((1, t, 2 * HEAD_DIM), lambda bi, ki, si: (bi, si, hk + ki)),
                  pl.BlockSpec((1, t, 2 * HEAD_DIM), lambda bi, ki, si: (bi, si, 2 * hk + ki)),
                  gate, gate, gate,
                  pl.BlockSpec((1, 1, 2, t), lambda bi, ki, si: (bi, ki, 0, si)),
                  pl.BlockSpec((1, HEAD_DIM), lambda bi, ki, si: (0, 0))],
        out_specs=pl.BlockSpec((1, t, 2 * HEAD_DIM), lambda bi, ki, si: (bi, si, ki)),
        out_shape=jax.ShapeDtypeStruct((b, s, 2 * hk * HEAD_DIM), BF16),
        scratch_shapes=[pltpu.VMEM((2, HEAD_DIM, HEAD_DIM), F32)],
        compiler_params=_params("parallel", "parallel", "arbitrary"),
        name="gdn_core",
    )(qkv_act, qkv_act, qkv_act, proj, beta, gc, gl, gct, norm_w)


def _router_kernel(logit_ref, o_ref, *, n_experts):
    x = logit_ref[...]
    lane = lax.broadcasted_iota(jnp.int32, x.shape, 1)
    x = jnp.where(lane < n_experts, x, -jnp.inf)
    m1 = jnp.max(x, axis=1, keepdims=True)
    i1 = jnp.min(jnp.where(x == m1, lane, LANES), axis=1, keepdims=True)
    x2 = jnp.where(lane == i1, -jnp.inf, x)
    m2 = jnp.max(x2, axis=1, keepdims=True)
    i2 = jnp.min(jnp.where(x2 == m2, lane, LANES), axis=1, keepdims=True)
    e2 = jnp.exp(m2 - m1)
    w1 = 1.0 / (1.0 + e2)
    w2 = e2 / (1.0 + e2)
    o_ref[...] = jnp.where(lane == i1, w1, 0.0) + jnp.where(lane == i2, w2, 0.0)


def router_combine(logits, n_experts, tm=1024):
    m, n = logits.shape
    return pl.pallas_call(
        functools.partial(_router_kernel, n_experts=n_experts),
        grid=(m // tm,),
        in_specs=[pl.BlockSpec((tm, n), lambda i: (i, 0))],
        out_specs=pl.BlockSpec((tm, n), lambda i: (i, 0)),
        out_shape=jax.ShapeDtypeStruct((m, n), F32),
        compiler_params=_params("parallel"),
        name="router",
    )(logits)


def _sb_layer(h, g, w_in, w_out):
    b, s, d = h.shape
    n_heads = w_out.shape[0] // HEAD_DIM
    hn = rmsnorm(h.reshape(b * s, d), g, BF16)
    qkv = matmul(hn, w_in.astype(BF16), BF16)
    o = sb_attention(qkv.reshape(b, s, -1), n_heads)
    return matmul(o.reshape(b * s, -1), w_out.astype(BF16), F32, res=h.reshape(b * s, d)).reshape(b, s, d)


def _fox_layer(h, g, w_in, b_f, w_out):
    b, s, d = h.shape
    n_heads = w_out.shape[0] // HEAD_DIM
    main = 4 * n_heads * HEAD_DIM
    h2 = h.reshape(b * s, d)
    hn = rmsnorm(h2, g, BF16)
    proj = matmul(hn, w_in[:, :main].astype(BF16), BF16).reshape(b, s, main)
    f_raw = small_proj(h2, g, _pad_cols(w_in[:, main:], LANES)).reshape(b, s, LANES)
    bias = jnp.pad(b_f, (0, LANES - n_heads)).reshape(1, LANES)
    cum_col, cum_t = fox_gate(f_raw, bias)
    o = fox_attention(proj, cum_col, cum_t[:, :n_heads, :], n_heads)
    return matmul(o.reshape(b * s, -1), w_out.astype(BF16), F32, res=h2).reshape(b, s, d)


def _gdn_layer(h, g, w_in, conv_w, a_log, dt_bias, norm_w, w_out):
    b, s, d = h.shape
    v_dim = w_out.shape[0]
    n_v_heads = v_dim // HEAD_DIM
    n_k_heads = n_v_heads // 2
    k_dim = n_k_heads * HEAD_DIM
    qkv_dim = 2 * k_dim + v_dim
    main = qkv_dim + v_dim
    h2 = h.reshape(b * s, d)
    hn = rmsnorm(h2, g, BF16)
    proj = matmul(hn, w_in[:, :main].astype(BF16), BF16).reshape(b, s, main)
    w_gates = jnp.concatenate([_pad_cols(w_in[:, main:main + n_v_heads], LANES),
                               _pad_cols(w_in[:, main + n_v_heads:], LANES)], axis=1)
    raw = small_proj(h2, g, w_gates).reshape(b, s, 2 * LANES)
    pad = (0, LANES - n_v_heads)
    beta, gc, gl, gct = gdn_gates(raw[..., :LANES], raw[..., LANES:],
                                  jnp.pad(a_log, pad).reshape(1, LANES), jnp.pad(dt_bias, pad).reshape(1, LANES))
    gct = gct[:, :n_v_heads, :].reshape(b, n_k_heads, 2, s)
    qkv_act = gdn_prep(proj, conv_w.T, qkv_dim, k_dim)
    o = gdn_core(qkv_act, proj, beta, gc, gl, gct, norm_w.reshape(1, HEAD_DIM), n_k_heads)
    return matmul(o.reshape(b * s, v_dim), w_out.astype(BF16), F32, res=h2).reshape(b, s, d)


def _dense_ffn(h, g, w_gate_up, w_down):
    b, s, d = h.shape
    h2 = h.reshape(b * s, d)
    hn = rmsnorm(h2, g, BF16)
    mid = swiglu_up(hn, w_gate_up.astype(BF16))
    return matmul(mid, w_down.astype(BF16), F32, res=h2).reshape(b, s, d)


def _moe_ffn(h, g, w_router, w_gate_up, w_down):
    b, s, d = h.shape
    n_experts = w_router.shape[1]
    h2 = h.reshape(b * s, d)
    hn = rmsnorm(h2, g, BF16)
    logits = small_proj(h2, g, _pad_cols(w_router, LANES))
    combine = router_combine(logits, n_experts)
    out = h2
    for e in range(n_experts):
        mid = swiglu_up(hn, w_gate_up[e].astype(BF16), tn=256)
        out = matmul(mid, w_down[e].astype(BF16), F32, res=out, scale=combine[:, e:e + 1], tk=256)
    return out.reshape(b, s, d)


def kernel(x, norm_mix, norm_ffn, sb_w_in, sb_w_out, gdn_w_in, gdn_conv_w, gdn_a_log, gdn_dt_bias, gdn_norm_w,
           gdn_w_out, fox_w_in, fox_b_f, fox_w_out, ffn_w_gate_up, ffn_w_down, moe_w_router, moe_w_gate_up,
           moe_w_down, final_norm):
    depth = norm_mix.shape[0]
    h = x
    for i in range(depth):
        kind, j = i % 3, i // 3
        if kind == 0:
            h = _sb_layer(h, norm_mix[i], sb_w_in[j], sb_w_out[j])
        elif kind == 1:
            h = _gdn_layer(h, norm_mix[i], gdn_w_in[j], gdn_conv_w[j], gdn_a_log[j], gdn_dt_bias[j],
                           gdn_norm_w[j], gdn_w_out[j])
        else:
            h = _fox_layer(h, norm_mix[i], fox_w_in[j], fox_b_f[j], fox_w_out[j])
        f = i // 2
        if i % 2 == 0:
            h = _dense_ffn(h, norm_ffn[i], ffn_w_gate_up[f], ffn_w_down[f])
        else:
            h = _moe_ffn(h, norm_ffn[i], moe_w_router[f], moe_w_gate_up[f], moe_w_down[f])
    b, s, d = h.shape
    return rmsnorm(h.reshape(b * s, d), final_norm, x.dtype).reshape(b, s, d)
```

```python
import functools

import jax
import jax.numpy as jnp
from jax import lax
from jax.experimental import pallas as pl
from jax.experimental.pallas import tpu as pltpu

HEAD_DIM = 128
LANES = 128
GDN_CONV = 4
GDN_CHUNK = 128
EPS = 1e-6
EXP_UNDERFLOW = -104.0
VMEM_LIMIT_BYTES = 56 * 1024 * 1024

F32 = jnp.float32
BF16 = jnp.bfloat16


def _params(*semantics):
    return pltpu.CompilerParams(dimension_semantics=semantics, vmem_limit_bytes=VMEM_LIMIT_BYTES)


def _tile(dim, preferred):
    assert dim % LANES == 0, dim
    t = min(preferred, dim) // LANES * LANES
    while dim % t:
        t -= LANES
    return t


def _dot(a, b):
    return jnp.dot(a, b, preferred_element_type=F32)


def _dot_nt(a, b):
    return lax.dot_general(a, b, (((1,), (1,)), ((), ())), preferred_element_type=F32)


def _split3(x):
    hi = x.astype(BF16)
    r = x - hi.astype(F32)
    mid = r.astype(BF16)
    lo = (r - mid.astype(F32)).astype(BF16)
    return hi, mid, lo


def _split2(x):
    hi = x.astype(BF16)
    lo = (x - hi.astype(F32)).astype(BF16)
    return hi, lo


def _softplus(z):
    return jnp.maximum(z, 0.0) + jnp.log(1.0 + jnp.exp(-jnp.abs(z)))


def _sigmoid(z):
    return 1.0 / (1.0 + jnp.exp(-z))


def _silu(z):
    return z * _sigmoid(z)


def _block_id(idx, size):
    shift = size.bit_length() - 1
    assert 1 << shift == size
    return jnp.right_shift(idx, shift)


def _rmsnorm_kernel(x_ref, g_ref, o_ref):
    x = x_ref[...]
    ms = jnp.mean(x * x, axis=-1, keepdims=True)
    o_ref[...] = (x * lax.rsqrt(ms + EPS) * g_ref[...]).astype(o_ref.dtype)


def rmsnorm(x, g, out_dtype, tm=512):
    m, d = x.shape
    return pl.pallas_call(
        _rmsnorm_kernel,
        grid=(m // tm,),
        in_specs=[pl.BlockSpec((tm, d), lambda i: (i, 0)),
                  pl.BlockSpec((1, d), lambda i: (0, 0))],
        out_specs=pl.BlockSpec((tm, d), lambda i: (i, 0)),
        out_shape=jax.ShapeDtypeStruct((m, d), out_dtype),
        compiler_params=_params("parallel"),
        name="rmsnorm",
    )(x, g.reshape(1, d))


def _matmul_kernel(*refs, nk, has_res):
    if has_res:
        a_ref, w_ref, res_ref, o_ref = refs[:4]
    else:
        a_ref, w_ref, o_ref = refs[:3]

    def finish(r):
        if has_res:
            r = r + res_ref[...]
        o_ref[...] = r.astype(o_ref.dtype)

    part = _dot(a_ref[...], w_ref[...])
    if nk == 1:
        finish(part)
        return
    acc_ref = refs[-1]
    k = pl.program_id(2)

    @pl.when(k == 0)
    def _():
        acc_ref[...] = part

    @pl.when((k > 0) & (k < nk - 1))
    def _():
        acc_ref[...] += part

    @pl.when(k == nk - 1)
    def _():
        finish(acc_ref[...] + part)


def matmul(a, w, out_dtype, res=None, tm=1024, tn=1024, tk=2048):
    m, kdim = a.shape
    n = w.shape[1]
    tm, tn, tk = _tile(m, tm), _tile(n, tn), _tile(kdim, tk)
    nk = kdim // tk
    in_specs = [pl.BlockSpec((tm, tk), lambda i, j, k: (i, k)),
                pl.BlockSpec((tk, tn), lambda i, j, k: (k, j))]
    args = [a, w]
    if res is not None:
        in_specs.append(pl.BlockSpec((tm, tn), lambda i, j, k: (i, j)))
        args.append(res)
    return pl.pallas_call(
        functools.partial(_matmul_kernel, nk=nk, has_res=res is not None),
        grid=(m // tm, n // tn, nk),
        in_specs=in_specs,
        out_specs=pl.BlockSpec((tm, tn), lambda i, j, k: (i, j)),
        out_shape=jax.ShapeDtypeStruct((m, n), out_dtype),
        scratch_shapes=[pltpu.VMEM((tm, tn), F32)] if nk > 1 else [],
        compiler_params=_params("parallel", "parallel", "arbitrary"),
        name="matmul",
    )(*args)


def _swiglu_up_kernel(a_ref, wg_ref, wu_ref, o_ref):
    a = a_ref[...]
    o_ref[...] = (_silu(_dot(a, wg_ref[...])) * _dot(a, wu_ref[...])).astype(o_ref.dtype)


def swiglu_up(a, w_gate_up, tm=1024, tn=512):
    m, kdim = a.shape
    f = w_gate_up.shape[1] // 2
    tm, tn = _tile(m, tm), _tile(f, tn)
    nj = f // tn
    return pl.pallas_call(
        _swiglu_up_kernel,
        grid=(m // tm, nj),
        in_specs=[pl.BlockSpec((tm, kdim), lambda i, j: (i, 0)),
                  pl.BlockSpec((kdim, tn), lambda i, j: (0, j)),
                  pl.BlockSpec((kdim, tn), lambda i, j: (0, j + nj))],
        out_specs=pl.BlockSpec((tm, tn), lambda i, j: (i, j)),
        out_shape=jax.ShapeDtypeStruct((m, f), BF16),
        compiler_params=_params("parallel", "parallel"),
        name="swiglu_up",
    )(a, w_gate_up, w_gate_up)


def _small_proj_kernel(h_ref, g_ref, w_ref, o_ref):
    x = h_ref[...]
    ms = jnp.mean(x * x, axis=-1, keepdims=True)
    xn = x * lax.rsqrt(ms + EPS) * g_ref[...]
    xh, xl = _split2(xn)
    wh, wl = _split2(w_ref[...])
    o_ref[...] = _dot(xh, wh) + _dot(xl, wh) + _dot(xh, wl)


def small_proj(h, g, w, tm=512):
    m, d = h.shape
    n = w.shape[1]
    return pl.pallas_call(
        _small_proj_kernel,
        grid=(m // tm,),
        in_specs=[pl.BlockSpec((tm, d), lambda i: (i, 0)),
                  pl.BlockSpec((1, d), lambda i: (0, 0)),
                  pl.BlockSpec((d, n), lambda i: (0, 0))],
        out_specs=pl.BlockSpec((tm, n), lambda i: (i, 0)),
        out_shape=jax.ShapeDtypeStruct((m, n), F32),
        compiler_params=_params("parallel"),
        name="small_proj",
    )(h, g.reshape(1, d), w)


def _pad_cols(w, width):
    return jnp.pad(w, ((0, 0), (0, width - w.shape[1])))


def _sb_attn_kernel(q_ref, k_ref, v_ref, o_ref, *, t, scale):
    i = pl.program_id(2)
    q = q_ref[0]
    row = lax.broadcasted_iota(jnp.int32, (t, t), 0)
    col = lax.broadcasted_iota(jnp.int32, (t, t), 1)
    strict = col < row
    suffix = jnp.where(row > col, 1.0, 0.0).astype(BF16)

    def block(kb, carry, acc, masked):
        start = pl.multiple_of(kb * t, t)
        k = k_ref[0, pl.ds(start, t), :]
        v = v_ref[0, pl.ds(start, t), :]
        z = _dot_nt(q, k) * scale
        sp = _softplus(z)
        log_keep = -sp
        if masked:
            log_keep = jnp.where(strict, log_keep, 0.0)
        hi, lo = _split2(log_keep)
        later = _dot(hi, suffix) + _dot(lo, suffix) + carry
        w = jnp.exp(z - sp + later)
        if masked:
            w = jnp.where(strict, w, 0.0)
        acc = acc + _dot(w.astype(BF16), v)
        carry = carry + jnp.sum(log_keep, axis=1, keepdims=True)
        return carry, acc

    carry, acc = block(i, jnp.zeros((t, 1), F32), jnp.zeros((t, HEAD_DIM), F32), True)

    def live(c):
        return (c[0] < i) & (c[3] > EXP_UNDERFLOW)

    def body(c):
        carry, acc = block(i - 1 - c[0], c[1], c[2], False)
        return c[0] + 1, carry, acc, jnp.max(carry)

    _, carry, acc, _ = lax.while_loop(live, body, (jnp.int32(0), carry, acc, jnp.max(carry)))
    o_ref[0] = acc.astype(o_ref.dtype)


def sb_attention(qkv, n_heads, t=256):
    b, s, _ = qkv.shape
    h = n_heads
    return pl.pallas_call(
        functools.partial(_sb_attn_kernel, t=t, scale=HEAD_DIM ** -0.5),
        grid=(b, h, s // t),
        in_specs=[pl.BlockSpec((1, t, HEAD_DIM), lambda bi, hi, i: (bi, i, hi)),
                  pl.BlockSpec((1, s, HEAD_DIM), lambda bi, hi, i: (bi, 0, h + hi)),
                  pl.BlockSpec((1, s, HEAD_DIM), lambda bi, hi, i: (bi, 0, 2 * h + hi))],
        out_specs=pl.BlockSpec((1, t, HEAD_DIM), lambda bi, hi, i: (bi, i, hi)),
        out_shape=jax.ShapeDtypeStruct((b, s, h * HEAD_DIM), BF16),
        compiler_params=_params("parallel", "parallel", "arbitrary"),
        name="sb_attention",
    )(qkv, qkv, qkv)


def _fox_gate_kernel(f_ref, bias_ref, col_ref, row_ref, carry_ref, *, t):
    i = pl.program_id(1)

    @pl.when(i == 0)
    def _():
        carry_ref[...] = jnp.zeros_like(carry_ref)

    x = f_ref[0] + bias_ref[...]
    log_f = -_softplus(-x)
    row = lax.broadcasted_iota(jnp.int32, (t, t), 0)
    col = lax.broadcasted_iota(jnp.int32, (t, t), 1)
    lower = jnp.where(col <= row, 1.0, 0.0).astype(BF16)
    hi, mid, lo = _split3(log_f)
    cum = _dot(lower, hi) + _dot(lower, mid) + _dot(lower, lo) + carry_ref[...]
    col_ref[0] = cum
    row_ref[0] = cum.T
    carry_ref[...] = cum[t - 1:t, :]


def fox_gate(f_raw, bias, t=512):
    b, s, n = f_raw.shape
    return pl.pallas_call(
        functools.partial(_fox_gate_kernel, t=t),
        grid=(b, s // t),
        in_specs=[pl.BlockSpec((1, t, n), lambda bi, i: (bi, i, 0)),
                  pl.BlockSpec((1, n), lambda bi, i: (0, 0))],
        out_specs=[pl.BlockSpec((1, t, n), lambda bi, i: (bi, i, 0)),
                   pl.BlockSpec((1, n, t), lambda bi, i: (bi, 0, i))],
        out_shape=[jax.ShapeDtypeStruct((b, s, n), F32), jax.ShapeDtypeStruct((b, n, s), F32)],
        scratch_shapes=[pltpu.VMEM((1, n), F32)],
        compiler_params=_params("parallel", "arbitrary"),
        name="fox_gate",
    )(f_raw, bias)


def _fox_attn_kernel(q_ref, k_ref, v_ref, gate_ref, cq_ref, ck_ref, o_ref, *, t, scale):
    hi = pl.program_id(1)
    i = pl.program_id(2)
    q = q_ref[0]
    lane = lax.broadcasted_iota(jnp.int32, (t, LANES), 1)
    cum_q = jnp.sum(jnp.where(lane == hi, cq_ref[0], 0.0), axis=1, keepdims=True)
    row = lax.broadcasted_iota(jnp.int32, (t, t), 0)
    col = lax.broadcasted_iota(jnp.int32, (t, t), 1)
    causal = col <= row

    def block(kb, m, l, acc, masked):
        start = pl.multiple_of(kb * t, t)
        k = k_ref[0, pl.ds(start, t), :]
        v = v_ref[0, pl.ds(start, t), :]
        cum_k = ck_ref[0, 0, pl.ds(kb, 1), :]
        logits = _dot_nt(q, k) * scale + cum_q - cum_k
        if masked:
            logits = jnp.where(causal, logits, -jnp.inf)
        m_new = jnp.maximum(m, jnp.max(logits, axis=1, keepdims=True))
        alpha = jnp.exp(m - m_new)
        p = jnp.exp(logits - m_new)
        l = l * alpha + jnp.sum(p, axis=1, keepdims=True)
        acc = acc * alpha + _dot(p.astype(BF16), v)
        return m_new, l, acc

    init = (jnp.full((t, 1), -jnp.inf, F32), jnp.zeros((t, 1), F32), jnp.zeros((t, HEAD_DIM), F32))
    m, l, acc = block(i, *init, True)

    def body(n, c):
        return block(i - 1 - n, *c, False)

    m, l, acc = lax.fori_loop(0, i, body, (m, l, acc))
    out = acc / l * _sigmoid(gate_ref[0].astype(F32))
    o_ref[0] = out.astype(o_ref.dtype)


def fox_attention(proj, cum_col, cum_row, n_heads, t=512):
    b, s, _ = proj.shape
    h = n_heads
    cum_row = cum_row.reshape(b, h, s // t, t)
    return pl.pallas_call(
        functools.partial(_fox_attn_kernel, t=t, scale=HEAD_DIM ** -0.5),
        grid=(b, h, s // t),
        in_specs=[pl.BlockSpec((1, t, HEAD_DIM), lambda bi, hi, i: (bi, i, hi)),
                  pl.BlockSpec((1, s, HEAD_DIM), lambda bi, hi, i: (bi, 0, h + hi)),
                  pl.BlockSpec((1, s, HEAD_DIM), lambda bi, hi, i: (bi, 0, 2 * h + hi)),
                  pl.BlockSpec((1, t, HEAD_DIM), lambda bi, hi, i: (bi, i, 3 * h + hi)),
                  pl.BlockSpec((1, t, LANES), lambda bi, hi, i: (bi, i, 0)),
                  pl.BlockSpec((1, 1, s // t, t), lambda bi, hi, i: (bi, hi, 0, 0))],
        out_specs=pl.BlockSpec((1, t, HEAD_DIM), lambda bi, hi, i: (bi, i, hi)),
        out_shape=jax.ShapeDtypeStruct((b, s, h * HEAD_DIM), BF16),
        compiler_params=_params("parallel", "parallel", "arbitrary"),
        name="fox_attention",
    )(proj, proj, proj, proj, cum_col, cum_row)


def _gdn_gate_kernel(b_ref, a_ref, alog_ref, dt_ref, beta_ref, gc_ref, gl_ref, gct_ref, *, t, chunk):
    beta_ref[0] = _sigmoid(b_ref[0])
    g = -jnp.exp(alog_ref[...]) * _softplus(a_ref[0] + dt_ref[...])
    row = lax.broadcasted_iota(jnp.int32, (t, t), 0)
    col = lax.broadcasted_iota(jnp.int32, (t, t), 1)
    same = _block_id(row, chunk) == _block_id(col, chunk)
    lower = jnp.where(same & (col <= row), 1.0, 0.0).astype(BF16)
    whole = jnp.where(same, 1.0, 0.0).astype(BF16)
    hi, mid, lo = _split3(g)
    gc = _dot(lower, hi) + _dot(lower, mid) + _dot(lower, lo)
    gc_ref[0] = gc
    gl_ref[0] = _dot(whole, hi) + _dot(whole, mid) + _dot(whole, lo)
    gct_ref[0] = gc.T


def gdn_gates(b_raw, a_raw, a_log, dt_bias, t=512):
    b, s, n = b_raw.shape
    spec = pl.BlockSpec((1, t, n), lambda bi, i: (bi, i, 0))
    vec = pl.BlockSpec((1, n), lambda bi, i: (0, 0))
    return pl.pallas_call(
        functools.partial(_gdn_gate_kernel, t=t, chunk=GDN_CHUNK),
        grid=(b, s // t),
        in_specs=[spec, spec, vec, vec],
        out_specs=[spec, spec, spec, pl.BlockSpec((1, n, t), lambda bi, i: (bi, 0, i))],
        out_shape=[jax.ShapeDtypeStruct((b, s, n), F32)] * 3 + [jax.ShapeDtypeStruct((b, n, s), F32)],
        compiler_params=_params("parallel", "parallel"),
        name="gdn_gates",
    )(b_raw, a_raw, a_log, dt_bias)


def _gdn_prep_kernel(x_ref, prev_ref, w_ref, o_ref, *, ts, n_q_blocks, n_qk_blocks, heads_per_block):
    si = pl.program_id(1)
    ci = pl.program_id(2)
    cur = x_ref[0].astype(F32)
    prev = jnp.where(si == 0, 0.0, prev_ref[0].astype(F32))
    ext = jnp.concatenate([prev, cur], axis=0)
    w = w_ref[...]
    y = jnp.zeros_like(cur)
    for tap in range(GDN_CONV):
        off = 8 - (GDN_CONV - 1) + tap
        y = y + w[tap:tap + 1, :] * ext[off:off + ts, :]
    y = _silu(y)
    q_scale = jnp.where(ci < n_q_blocks, HEAD_DIM ** -0.5, 1.0)
    is_qk = ci < n_qk_blocks
    for hh in range(heads_per_block):
        sl = slice(hh * HEAD_DIM, (hh + 1) * HEAD_DIM)
        yh = y[:, sl]
        ss = jnp.sum(yh * yh, axis=1, keepdims=True)
        factor = jnp.where(is_qk, lax.rsqrt(ss + EPS) * q_scale, 1.0)
        o_ref[0, :, sl] = (yh * factor).astype(o_ref.dtype)


def gdn_prep(proj, conv_w_t, qkv_dim, k_dim, ts=512, tc=512):
    b, s, _ = proj.shape
    ts, tc = min(ts, s), min(tc, k_dim)
    rows8 = ts // 8
    return pl.pallas_call(
        functools.partial(_gdn_prep_kernel, ts=ts, n_q_blocks=k_dim // tc, n_qk_blocks=2 * k_dim // tc,
                          heads_per_block=tc // HEAD_DIM),
        grid=(b, s // ts, qkv_dim // tc),
        in_specs=[pl.BlockSpec((1, ts, tc), lambda bi, si, ci: (bi, si, ci)),
                  pl.BlockSpec((1, 8, tc), lambda bi, si, ci: (bi, jnp.maximum(si * rows8 - 1, 0), ci)),
                  pl.BlockSpec((GDN_CONV, tc), lambda bi, si, ci: (0, ci))],
        out_specs=pl.BlockSpec((1, ts, tc), lambda bi, si, ci: (bi, si, ci)),
        out_shape=jax.ShapeDtypeStruct((b, s, qkv_dim), BF16),
        compiler_params=_params("parallel", "parallel", "parallel"),
        name="gdn_prep",
    )(proj, proj, conv_w_t)


def _bmm(a, b):
    return lax.dot_general(a, b, (((2,), (1,)), ((0,), (0,))), preferred_element_type=F32)


def _bmm_nt(a, b):
    return lax.dot_general(a, b, (((2,), (2,)), ((0,), (0,))), preferred_element_type=F32)


def _bmm_tn(a, b):
    return lax.dot_general(a, b, (((1,), (1,)), ((0,), (0,))), preferred_element_type=F32)


def _inv_unit_lower(m_strict, row, col):
    c = m_strict.shape[-1]
    eye = jnp.where(row == col, 1.0, 0.0)
    n1 = jnp.where(_block_id(row, 8) == _block_id(col, 8), -m_strict, 0.0)
    n1b = n1.astype(BF16)
    n2 = _bmm(n1b, n1b)
    n2b = n2.astype(BF16)
    n4 = _bmm(n2b, n2b)
    x = _bmm((eye + n1).astype(BF16), (eye + n2).astype(BF16))
    x = _bmm(x.astype(BF16), (eye + n4).astype(BF16))
    size = 8
    while size < c:
        lower_left = ((_block_id(row, size) == _block_id(col, size) + 1)
                      & (_block_id(row, 2 * size) == _block_id(col, 2 * size)))
        cm = jnp.where(lower_left, m_strict, 0.0).astype(BF16)
        xb = x.astype(BF16)
        x = x - _bmm(_bmm(xb, cm).astype(BF16), xb)
        size *= 2
    return x


def _gdn_kernel(q_ref, k_ref, v_ref, z_ref, beta_ref, gc_ref, gl_ref, gct_ref, nw_ref, o_ref, state_ref,
                *, n_chunks):
    ki = pl.program_id(1)
    si = pl.program_id(2)
    c = GDN_CHUNK
    t = n_chunks * c

    @pl.when(si == 0)
    def _():
        state_ref[...] = jnp.zeros_like(state_ref)

    lane = lax.broadcasted_iota(jnp.int32, (t, LANES), 1)
    row = lax.broadcasted_iota(jnp.int32, (c, c), 0)
    col = lax.broadcasted_iota(jnp.int32, (c, c), 1)
    strict = col < row
    incl = col <= row

    nc = n_chunks

    def column(ref, head):
        col_vec = jnp.sum(jnp.where(lane == head, ref[0], 0.0), axis=1, keepdims=True)
        return col_vec.reshape(nc, c, 1)

    def both(per_head):
        return jnp.concatenate([per_head(0), per_head(1)], axis=0)

    beta = both(lambda hh: column(beta_ref, 2 * ki + hh))
    gcum = both(lambda hh: column(gc_ref, 2 * ki + hh))
    glast = both(lambda hh: column(gl_ref, 2 * ki + hh))
    g_row = both(lambda hh: gct_ref[0, 0, :, hh:hh + 1, :])
    q = q_ref[0].reshape(nc, c, HEAD_DIM)
    k = k_ref[0].reshape(nc, c, HEAD_DIM)
    kk = _bmm_nt(k, k)
    qk = _bmm_nt(q, k)
    qf = both(lambda hh: q.astype(F32))
    kf = both(lambda hh: k.astype(F32))
    v = both(lambda hh: v_ref[0, :, hh * HEAD_DIM:(hh + 1) * HEAD_DIM].reshape(nc, c, HEAD_DIM)).astype(F32)

    decay = jnp.exp(jnp.minimum(gcum - g_row, 0.0))
    m_strict = jnp.where(strict, both(lambda hh: kk) * beta * decay, 0.0)
    attn = jnp.where(incl, both(lambda hh: qk) * decay, 0.0).astype(BF16)
    t_inv = _inv_unit_lower(m_strict, row, col).astype(BF16)
    u = _bmm(t_inv, (v * beta).astype(BF16))
    w = _bmm(t_inv, (kf * (beta * jnp.exp(gcum))).astype(BF16)).astype(BF16)
    q_dec = (qf * jnp.exp(gcum)).astype(BF16)
    k_dec = (kf * jnp.exp(glast - gcum)).astype(BF16)
    chunk_decay = jnp.exp(glast[:, 0:1, :])
    nw = nw_ref[...]

    state = state_ref[...]
    for ch in range(nc):
        def pick(x):
            return jnp.concatenate([x[ch:ch + 1], x[nc + ch:nc + ch + 1]], axis=0)

        state_b = state.astype(BF16)
        v_new = pick(u) - _bmm(pick(w), state_b)
        v_new_b = v_new.astype(BF16)
        o = _bmm(pick(q_dec), state_b) + _bmm(pick(attn), v_new_b)
        state = state * pick(chunk_decay) + _bmm_tn(pick(k_dec), v_new_b)
        ms = jnp.mean(o * o, axis=2, keepdims=True)
        rows = slice(ch * c, (ch + 1) * c)
        zf = z_ref[0, rows, :].astype(F32)
        for hh in range(2):
            hs = slice(hh * HEAD_DIM, (hh + 1) * HEAD_DIM)
            o_ref[0, rows, hs] = (o[hh] * lax.rsqrt(ms[hh] + EPS) * nw * _silu(zf[:, hs])).astype(o_ref.dtype)
    state_ref[...] = state


def gdn_core(qkv_act, proj, beta, gc, gl, gct, norm_w, n_k_heads, n_chunks=4):
    b, s, _ = qkv_act.shape
    hk = n_k_heads
    n_chunks = min(n_chunks, s // GDN_CHUNK)
    t = n_chunks * GDN_CHUNK
    gate = pl.BlockSpec((1, t, LANES), lambda bi, ki, si: (bi, si, 0))
    return pl.pallas_call(
        functools.partial(_gdn_kernel, n_chunks=n_chunks),
        grid=(b, hk, s // t),
        in_specs=[pl.BlockSpec((1, t, HEAD_DIM), lambda bi, ki, si: (bi, si, ki)),
                  pl.BlockSpec((1, t, HEAD_DIM), lambda bi, ki, si: (bi, si, hk + ki)),
                  pl.BlockSpec((1, t, 2 * HEAD_DIM), lambda bi, ki, si: (bi, si, hk + ki)),
                  pl.BlockSpec((1, t, 2 * HEAD_DIM), lambda bi, ki, si: (bi, si, 2 * hk + ki)),
                  gate, gate, gate,
                  pl.BlockSpec((1, 1, n_chunks, 2, GDN_CHUNK), lambda bi, ki, si: (bi, ki, si, 0, 0)),
                  pl.BlockSpec((1, HEAD_DIM), lambda bi, ki, si: (0, 0))],
        out_specs=pl.BlockSpec((1, t, 2 * HEAD_DIM), lambda bi, ki, si: (bi, si, ki)),
        out_shape=jax.ShapeDtypeStruct((b, s, 2 * hk * HEAD_DIM), BF16),
        scratch_shapes=[pltpu.VMEM((2, HEAD_DIM, HEAD_DIM), F32)],
        compiler_params=_params("parallel", "parallel", "arbitrary"),
        name="gdn_core",
    )(qkv_act, qkv_act, qkv_act, proj, beta, gc, gl, gct, norm_w)


def _router_kernel(logit_ref, idx_ref, w_ref, *, n_experts):
    x = logit_ref[...]
    lane = lax.broadcasted_iota(jnp.int32, x.shape, 1)
    x = jnp.where(lane < n_experts, x, -jnp.inf)
    m1 = jnp.max(x, axis=1, keepdims=True)
    i1 = jnp.min(jnp.where(x == m1, lane, LANES), axis=1, keepdims=True)
    x2 = jnp.where(lane == i1, -jnp.inf, x)
    m2 = jnp.max(x2, axis=1, keepdims=True)
    i2 = jnp.min(jnp.where(x2 == m2, lane, LANES), axis=1, keepdims=True)
    e2 = jnp.exp(m2 - m1)
    w1 = 1.0 / (1.0 + e2)
    w2 = e2 / (1.0 + e2)
    idx_ref[...] = jnp.where(lane == 0, i1, jnp.where(lane == 1, i2, 0))
    w_ref[...] = jnp.where(lane == 0, w1, jnp.where(lane == 1, w2, 0.0))


def router_top2(logits, n_experts, tm=1024):
    m, n = logits.shape
    spec = pl.BlockSpec((tm, n), lambda i: (i, 0))
    return pl.pallas_call(
        functools.partial(_router_kernel, n_experts=n_experts),
        grid=(m // tm,),
        in_specs=[spec],
        out_specs=[spec, spec],
        out_shape=[jax.ShapeDtypeStruct((m, n), jnp.int32), jax.ShapeDtypeStruct((m, n), F32)],
        compiler_params=_params("parallel"),
        name="router",
    )(logits)


MOE_ROW_TILE = 256


def _route(idx, n_experts, tm):
    m = idx.shape[0]
    a = 2 * m
    e = idx.reshape(a)
    onehot = (e[:, None] == jnp.arange(n_experts, dtype=jnp.int32)[None, :]).astype(jnp.int32)
    csum = jnp.cumsum(onehot, axis=0)
    rank = jnp.take_along_axis(csum, e[:, None], axis=1)[:, 0] - 1
    counts = csum[-1]
    padded = (counts + tm - 1) // tm * tm
    ends = jnp.cumsum(padded)
    pos = (ends - padded)[e] + rank
    n_tiles = a // tm + n_experts
    tile_expert = jnp.minimum(jnp.searchsorted(ends, jnp.arange(n_tiles, dtype=jnp.int32) * tm, side="right"),
                              n_experts - 1).astype(jnp.int32)
    n_used = (ends[-1] // tm).astype(jnp.int32).reshape(1)
    row_token = jnp.zeros((n_tiles * tm,), jnp.int32).at[pos].set(jnp.arange(a, dtype=jnp.int32) // 2)
    return pos.astype(jnp.int32), tile_expert, n_used, row_token


def _row_copy(src_hbm, src_row, dst_vmem, dst_row, sem):
    return pltpu.make_async_copy(src_hbm.at[pl.ds(src_row, 1)], dst_vmem.at[pl.ds(dst_row, 1)], sem)


def _moe_gather_kernel(row_token_ref, n_used_ref, h_hbm, g_ref, o_ref, buf, sem, *, tm):
    i = pl.program_id(0)

    @pl.when(i < n_used_ref[0])
    def _():
        def start(r, c):
            _row_copy(h_hbm, row_token_ref[i * tm + r], buf, r, sem).start()
            return c

        lax.fori_loop(0, tm, start, 0)

        def wait(r, c):
            _row_copy(h_hbm, 0, buf, r, sem).wait()
            return c

        lax.fori_loop(0, tm, wait, 0)
        x = buf[...]
        ms = jnp.mean(x * x, axis=-1, keepdims=True)
        o_ref[...] = (x * lax.rsqrt(ms + EPS) * g_ref[...]).astype(o_ref.dtype)

    @pl.when(i >= n_used_ref[0])
    def _():
        o_ref[...] = jnp.zeros_like(o_ref)


def moe_gather(h, g, row_token, n_used, tm):
    m, d = h.shape
    p = row_token.shape[0]
    return pl.pallas_call(
        functools.partial(_moe_gather_kernel, tm=tm),
        grid_spec=pltpu.PrefetchScalarGridSpec(
            num_scalar_prefetch=2,
            grid=(p // tm,),
            in_specs=[pl.BlockSpec(memory_space=pl.ANY),
                      pl.BlockSpec((1, d), lambda i, rt, nu: (0, 0))],
            out_specs=pl.BlockSpec((tm, d), lambda i, rt, nu: (i, 0)),
            scratch_shapes=[pltpu.VMEM((tm, d), F32), pltpu.SemaphoreType.DMA(())]),
        out_shape=jax.ShapeDtypeStruct((p, d), BF16),
        compiler_params=_params("arbitrary"),
        name="moe_gather",
    )(row_token, n_used, h, g.reshape(1, d))


def _moe_up_kernel(te_ref, n_used_ref, x_ref, wg_ref, wu_ref, o_ref):
    i = pl.program_id(1)

    @pl.when(i < n_used_ref[0])
    def _():
        x = x_ref[...]
        gate = _dot(x, wg_ref[0])
        up = _dot(x, wu_ref[0])
        o_ref[...] = (_silu(gate) * up).astype(o_ref.dtype)

    @pl.when(i >= n_used_ref[0])
    def _():
        o_ref[...] = jnp.zeros_like(o_ref)


def moe_up(xs, w_gate_up, tile_expert, n_used, tm, tn=1408):
    p, d = xs.shape
    f = w_gate_up.shape[2] // 2
    tn = _tile(f, tn)
    nj = f // tn

    def row_tile(j, i, te, nu):
        return (jnp.minimum(i, nu[0] - 1), 0)

    return pl.pallas_call(
        _moe_up_kernel,
        grid_spec=pltpu.PrefetchScalarGridSpec(
            num_scalar_prefetch=2,
            grid=(nj, p // tm),
            in_specs=[pl.BlockSpec((tm, d), row_tile),
                      pl.BlockSpec((1, d, tn), lambda j, i, te, nu: (te[i], 0, j)),
                      pl.BlockSpec((1, d, tn), lambda j, i, te, nu: (te[i], 0, j + nj))],
            out_specs=pl.BlockSpec((tm, tn), lambda j, i, te, nu: (i, j))),
        out_shape=jax.ShapeDtypeStruct((p, f), BF16),
        compiler_params=_params("parallel", "arbitrary"),
        name="moe_up",
    )(tile_expert, n_used, xs, w_gate_up, w_gate_up)


def _moe_down_kernel(te_ref, n_used_ref, x_ref, w_ref, o_ref):
    i = pl.program_id(0)

    @pl.when(i < n_used_ref[0])
    def _():
        o_ref[...] = _dot(x_ref[...], w_ref[0])

    @pl.when(i >= n_used_ref[0])
    def _():
        o_ref[...] = jnp.zeros_like(o_ref)


def moe_down(hs, w_down, tile_expert, n_used, tm):
    p, f = hs.shape
    d = w_down.shape[2]
    return pl.pallas_call(
        _moe_down_kernel,
        grid_spec=pltpu.PrefetchScalarGridSpec(
            num_scalar_prefetch=2,
            grid=(p // tm,),
            in_specs=[pl.BlockSpec((tm, f), lambda i, te, nu: (jnp.minimum(i, nu[0] - 1), 0)),
                      pl.BlockSpec((1, f, d), lambda i, te, nu: (te[i], 0, 0))],
            out_specs=pl.BlockSpec((tm, d), lambda i, te, nu: (i, 0))),
        out_shape=jax.ShapeDtypeStruct((p, d), F32),
        compiler_params=_params("arbitrary"),
        name="moe_down",
    )(tile_expert, n_used, hs, w_down)


def _moe_combine_kernel(pos_ref, h_ref, w_ref, y_hbm, o_ref, buf, sem, *, tm):
    i = pl.program_id(0)

    def start(r, c):
        for k in range(2):
            _row_copy(y_hbm, pos_ref[2 * (i * tm + r) + k], buf.at[k], r, sem).start()
        return c

    lax.fori_loop(0, tm, start, 0)

    def wait(r, c):
        for k in range(2):
            _row_copy(y_hbm, 0, buf.at[k], r, sem).wait()
        return c

    lax.fori_loop(0, tm, wait, 0)
    w = w_ref[...]
    o_ref[...] = h_ref[...] + w[:, 0:1] * buf[0] + w[:, 1:2] * buf[1]


def moe_combine(h, top_w, ys, pos, tm=256):
    m, d = h.shape
    return pl.pallas_call(
        functools.partial(_moe_combine_kernel, tm=tm),
        grid_spec=pltpu.PrefetchScalarGridSpec(
            num_scalar_prefetch=1,
            grid=(m // tm,),
            in_specs=[pl.BlockSpec((tm, d), lambda i, pos: (i, 0)),
                      pl.BlockSpec((tm, LANES), lambda i, pos: (i, 0)),
                      pl.BlockSpec(memory_space=pl.ANY)],
            out_specs=pl.BlockSpec((tm, d), lambda i, pos: (i, 0)),
            scratch_shapes=[pltpu.VMEM((2, tm, d), F32), pltpu.SemaphoreType.DMA(())]),
        out_shape=jax.ShapeDtypeStruct((m, d), F32),
        compiler_params=_params("arbitrary"),
        name="moe_combine",
    )(pos, h, top_w, ys)


def _sb_layer(h, g, w_in, w_out):
    b, s, d = h.shape
    n_heads = w_out.shape[0] // HEAD_DIM
    hn = rmsnorm(h.reshape(b * s, d), g, BF16)
    qkv = matmul(hn, w_in.astype(BF16), BF16)
    o = sb_attention(qkv.reshape(b, s, -1), n_heads)
    return matmul(o.reshape(b * s, -1), w_out.astype(BF16), F32, res=h.reshape(b * s, d)).reshape(b, s, d)


def _fox_layer(h, g, w_in, b_f, w_out):
    b, s, d = h.shape
    n_heads = w_out.shape[0] // HEAD_DIM
    main = 4 * n_heads * HEAD_DIM
    h2 = h.reshape(b * s, d)
    hn = rmsnorm(h2, g, BF16)
    proj = matmul(hn, w_in[:, :main].astype(BF16), BF16).reshape(b, s, main)
    f_raw = small_proj(h2, g, _pad_cols(w_in[:, main:], LANES)).reshape(b, s, LANES)
    bias = jnp.pad(b_f, (0, LANES - n_heads)).reshape(1, LANES)
    cum_col, cum_t = fox_gate(f_raw, bias)
    o = fox_attention(proj, cum_col, cum_t[:, :n_heads, :], n_heads)
    return matmul(o.reshape(b * s, -1), w_out.astype(BF16), F32, res=h2).reshape(b, s, d)


def _gdn_layer(h, g, w_in, conv_w, a_log, dt_bias, norm_w, w_out):
    b, s, d = h.shape
    v_dim = w_out.shape[0]
    n_v_heads = v_dim // HEAD_DIM
    n_k_heads = n_v_heads // 2
    k_dim = n_k_heads * HEAD_DIM
    qkv_dim = 2 * k_dim + v_dim
    main = qkv_dim + v_dim
    h2 = h.reshape(b * s, d)
    hn = rmsnorm(h2, g, BF16)
    proj = matmul(hn, w_in[:, :main].astype(BF16), BF16).reshape(b, s, main)
    w_gates = jnp.concatenate([_pad_cols(w_in[:, main:main + n_v_heads], LANES),
                               _pad_cols(w_in[:, main + n_v_heads:], LANES)], axis=1)
    raw = small_proj(h2, g, w_gates).reshape(b, s, 2 * LANES)
    pad = (0, LANES - n_v_heads)
    beta, gc, gl, gct = gdn_gates(raw[..., :LANES], raw[..., LANES:],
                                  jnp.pad(a_log, pad).reshape(1, LANES), jnp.pad(dt_bias, pad).reshape(1, LANES))
    gct = gct[:, :n_v_heads, :].reshape(b, n_k_heads, 2, s // GDN_CHUNK, GDN_CHUNK).transpose(0, 1, 3, 2, 4)
    qkv_act = gdn_prep(proj, conv_w.T, qkv_dim, k_dim)
    o = gdn_core(qkv_act, proj, beta, gc, gl, gct, norm_w.reshape(1, HEAD_DIM), n_k_heads)
    return matmul(o.reshape(b * s, v_dim), w_out.astype(BF16), F32, res=h2).reshape(b, s, d)


def _dense_ffn(h, g, w_gate_up, w_down):
    b, s, d = h.shape
    h2 = h.reshape(b * s, d)
    hn = rmsnorm(h2, g, BF16)
    mid = swiglu_up(hn, w_gate_up.astype(BF16))
    return matmul(mid, w_down.astype(BF16), F32, res=h2).reshape(b, s, d)


def _moe_ffn(h, g, w_router, w_gate_up, w_down):
    b, s, d = h.shape
    n_experts = w_router.shape[1]
    h2 = h.reshape(b * s, d)
    logits = small_proj(h2, g, _pad_cols(w_router, LANES))
    top_idx, top_w = router_top2(logits, n_experts)
    tm = MOE_ROW_TILE
    pos, tile_expert, n_used, row_token = _route(top_idx[:, :2], n_experts, tm)
    xs = moe_gather(h2, g, row_token, n_used, tm)
    mid = moe_up(xs, w_gate_up.astype(BF16), tile_expert, n_used, tm)
    ys = moe_down(mid, w_down.astype(BF16), tile_expert, n_used, tm)
    return moe_combine(h2, top_w, ys, pos).reshape(b, s, d)


def kernel(x, norm_mix, norm_ffn, sb_w_in, sb_w_out, gdn_w_in, gdn_conv_w, gdn_a_log, gdn_dt_bias, gdn_norm_w,
           gdn_w_out, fox_w_in, fox_b_f, fox_w_out, ffn_w_gate_up, ffn_w_down, moe_w_router, moe_w_gate_up,
           moe_w_down, final_norm):
    depth = norm_mix.shape[0]
    h = x
    for i in range(depth):
        kind, j = i % 3, i // 3
        if kind == 0:
            h = _sb_layer(h, norm_mix[i], sb_w_in[j], sb_w_out[j])
        elif kind == 1:
            h = _gdn_layer(h, norm_mix[i], gdn_w_in[j], gdn_conv_w[j], gdn_a_log[j], gdn_dt_bias[j],
                           gdn_norm_w[j], gdn_w_out[j])
        else:
            h = _fox_layer(h, norm_mix[i], fox_w_in[j], fox_b_f[j], fox_w_out[j])
        f = i // 2
        if i % 2 == 0:
            h = _dense_ffn(h, norm_ffn[i], ffn_w_gate_up[f], ffn_w_down[f])
        else:
            h = _moe_ffn(h, norm_ffn[i], moe_w_router[f], moe_w_gate_up[f], moe_w_down[f])
    b, s, d = h.shape
    return rmsnorm(h.reshape(b * s, d), final_norm, x.dtype).reshape(b, s, d)
```

```python
import functools

import jax
import jax.numpy as jnp
from jax import lax
from jax.experimental import pallas as pl
from jax.experimental.pallas import tpu as pltpu

HEAD_DIM = 128
LANES = 128
GDN_CONV = 4
GDN_CHUNK = 128
EPS = 1e-6
EXP_UNDERFLOW = -104.0
VMEM_LIMIT_BYTES = 56 * 1024 * 1024

F32 = jnp.float32
BF16 = jnp.bfloat16


def _params(*semantics):
    return pltpu.CompilerParams(dimension_semantics=semantics, vmem_limit_bytes=VMEM_LIMIT_BYTES)


def _tile(dim, preferred):
    assert dim % LANES == 0, dim
    t = min(preferred, dim) // LANES * LANES
    while dim % t:
        t -= LANES
    return t


def _dot(a, b):
    return jnp.dot(a, b, preferred_element_type=F32)


def _split3(x):
    hi = x.astype(BF16)
    r = x - hi.astype(F32)
    mid = r.astype(BF16)
    lo = (r - mid.astype(F32)).astype(BF16)
    return hi, mid, lo


def _split2(x):
    hi = x.astype(BF16)
    lo = (x - hi.astype(F32)).astype(BF16)
    return hi, lo


def _softplus(z):
    return jnp.maximum(z, 0.0) + jnp.log(1.0 + jnp.exp(-jnp.abs(z)))


def _sigmoid(z):
    return 1.0 / (1.0 + jnp.exp(-z))


def _silu(z):
    return z * _sigmoid(z)


def _block_id(idx, size):
    shift = size.bit_length() - 1
    assert 1 << shift == size
    return jnp.right_shift(idx, shift)


def _rmsnorm_kernel(x_ref, g_ref, o_ref):
    x = x_ref[...]
    ms = jnp.mean(x * x, axis=-1, keepdims=True)
    o_ref[...] = (x * lax.rsqrt(ms + EPS) * g_ref[...]).astype(o_ref.dtype)


def rmsnorm(x, g, out_dtype, tm=512):
    m, d = x.shape
    return pl.pallas_call(
        _rmsnorm_kernel,
        grid=(m // tm,),
        in_specs=[pl.BlockSpec((tm, d), lambda i: (i, 0)),
                  pl.BlockSpec((1, d), lambda i: (0, 0))],
        out_specs=pl.BlockSpec((tm, d), lambda i: (i, 0)),
        out_shape=jax.ShapeDtypeStruct((m, d), out_dtype),
        compiler_params=_params("parallel"),
        name="rmsnorm",
    )(x, g.reshape(1, d))


WS_VMEM_BUDGET = 44 * 1024 * 1024


def _ws_tiles(m, kdim, n, n_weight_blocks, out_bytes, has_res):
    for tm_pref, tn_pref in ((1024, 1024), (1024, 512), (512, 512), (512, 256), (256, 256), (256, 128)):
        tm, tn = _tile(m, tm_pref), _tile(n, tn_pref)
        weights = n_weight_blocks * kdim * tn * (2 * 4 + 2)
        acts = 2 * tm * kdim * 2
        outs = 2 * tm * tn * (out_bytes + (4 if has_res else 0))
        results = n_weight_blocks * tm * tn * 4
        if weights + acts + outs + results <= WS_VMEM_BUDGET:
            return tm, tn
    raise ValueError(f"no weight-stationary tiling fits VMEM for {(m, kdim, n)}")


def _matmul_kernel(*refs, has_res):
    if has_res:
        a_ref, w_ref, res_ref, o_ref, wb_ref = refs
    else:
        a_ref, w_ref, o_ref, wb_ref = refs

    @pl.when(pl.program_id(1) == 0)
    def _():
        wb_ref[...] = w_ref[...].astype(BF16)

    r = _dot(a_ref[...], wb_ref[...])
    if has_res:
        r = r + res_ref[...]
    o_ref[...] = r.astype(o_ref.dtype)


def matmul(a, w, layer, n_cols, out_dtype, res=None):
    m, kdim = a.shape
    tm, tn = _ws_tiles(m, kdim, n_cols, 1, jnp.dtype(out_dtype).itemsize, res is not None)
    in_specs = [pl.BlockSpec((tm, kdim), lambda j, i: (i, 0)),
                pl.BlockSpec((None, kdim, tn), lambda j, i: (layer, 0, j))]
    args = [a, w]
    if res is not None:
        in_specs.append(pl.BlockSpec((tm, tn), lambda j, i: (i, j)))
        args.append(res)
    return pl.pallas_call(
        functools.partial(_matmul_kernel, has_res=res is not None),
        grid=(n_cols // tn, m // tm),
        in_specs=in_specs,
        out_specs=pl.BlockSpec((tm, tn), lambda j, i: (i, j)),
        out_shape=jax.ShapeDtypeStruct((m, n_cols), out_dtype),
        scratch_shapes=[pltpu.VMEM((kdim, tn), BF16)],
        compiler_params=_params("parallel", "arbitrary"),
        name="matmul",
    )(*args)


def _swiglu_up_kernel(a_ref, wg_ref, wu_ref, o_ref, wgb_ref, wub_ref):
    @pl.when(pl.program_id(1) == 0)
    def _():
        wgb_ref[...] = wg_ref[...].astype(BF16)
        wub_ref[...] = wu_ref[...].astype(BF16)

    a = a_ref[...]
    o_ref[...] = (_silu(_dot(a, wgb_ref[...])) * _dot(a, wub_ref[...])).astype(o_ref.dtype)


def swiglu_up(a, w_gate_up, layer):
    m, kdim = a.shape
    f = w_gate_up.shape[2] // 2
    tm, tn = _ws_tiles(m, kdim, f, 2, 2, False)
    nj = f // tn
    return pl.pallas_call(
        _swiglu_up_kernel,
        grid=(nj, m // tm),
        in_specs=[pl.BlockSpec((tm, kdim), lambda j, i: (i, 0)),
                  pl.BlockSpec((None, kdim, tn), lambda j, i: (layer, 0, j)),
                  pl.BlockSpec((None, kdim, tn), lambda j, i: (layer, 0, j + nj))],
        out_specs=pl.BlockSpec((tm, tn), lambda j, i: (i, j)),
        out_shape=jax.ShapeDtypeStruct((m, f), BF16),
        scratch_shapes=[pltpu.VMEM((kdim, tn), BF16), pltpu.VMEM((kdim, tn), BF16)],
        compiler_params=_params("parallel", "arbitrary"),
        name="swiglu_up",
    )(a, w_gate_up, w_gate_up)


def _small_proj_kernel(h_ref, g_ref, w_ref, o_ref):
    x = h_ref[...]
    ms = jnp.mean(x * x, axis=-1, keepdims=True)
    xn = x * lax.rsqrt(ms + EPS) * g_ref[...]
    xh, xl = _split2(xn)
    wh, wl = _split2(w_ref[...])
    o_ref[...] = _dot(xh, wh) + _dot(xl, wh) + _dot(xh, wl)


def small_proj(h, g, w, tm=512):
    m, d = h.shape
    n = w.shape[1]
    return pl.pallas_call(
        _small_proj_kernel,
        grid=(m // tm,),
        in_specs=[pl.BlockSpec((tm, d), lambda i: (i, 0)),
                  pl.BlockSpec((1, d), lambda i: (0, 0)),
                  pl.BlockSpec((d, n), lambda i: (0, 0))],
        out_specs=pl.BlockSpec((tm, n), lambda i: (i, 0)),
        out_shape=jax.ShapeDtypeStruct((m, n), F32),
        compiler_params=_params("parallel"),
        name="small_proj",
    )(h, g.reshape(1, d), w)


def _pad_cols(w, width):
    return jnp.pad(w, ((0, 0), (0, width - w.shape[1])))


def _split_heads(x, n):
    return jnp.stack([x[:, h * HEAD_DIM:(h + 1) * HEAD_DIM] for h in range(n)], axis=0)


def _sb_attn_kernel(q_ref, k_ref, v_ref, o_ref, *, t, hb, scale):
    i = pl.program_id(2)
    q = _split_heads(q_ref[0], hb)
    row = lax.broadcasted_iota(jnp.int32, (t, t), 0)
    col = lax.broadcasted_iota(jnp.int32, (t, t), 1)
    strict = col < row
    suffix = jnp.where(row > col, 1.0, 0.0).astype(BF16)

    def block(kb, carry, acc, masked):
        start = pl.multiple_of(kb * t, t)
        k = _split_heads(k_ref[0, pl.ds(start, t), :], hb)
        v = _split_heads(v_ref[0, pl.ds(start, t), :], hb)
        z = _bmm_nt(q, k) * scale
        sp = _softplus(z)
        log_keep = -sp
        if masked:
            log_keep = jnp.where(strict, log_keep, 0.0)
        hi, lo = _split2(log_keep.reshape(hb * t, t))
        later = (_dot(hi, suffix) + _dot(lo, suffix)).reshape(hb, t, t) + carry
        w = jnp.exp(z - sp + later)
        if masked:
            w = jnp.where(strict, w, 0.0)
        acc = acc + _bmm(w.astype(BF16), v)
        carry = carry + jnp.sum(log_keep, axis=2, keepdims=True)
        return carry, acc

    carry, acc = block(i, jnp.zeros((hb, t, 1), F32), jnp.zeros((hb, t, HEAD_DIM), F32), True)

    def live(c):
        return (c[0] < i) & (c[3] > EXP_UNDERFLOW)

    def body(c):
        carry, acc = block(i - 1 - c[0], c[1], c[2], False)
        return c[0] + 1, carry, acc, jnp.max(carry)

    _, carry, acc, _ = lax.while_loop(live, body, (jnp.int32(0), carry, acc, jnp.max(carry)))
    for h in range(hb):
        o_ref[0, :, h * HEAD_DIM:(h + 1) * HEAD_DIM] = acc[h].astype(o_ref.dtype)


def sb_attention(qkv, n_heads, t=256, hb=4):
    b, s, _ = qkv.shape
    hb = min(hb, n_heads)
    assert n_heads % hb == 0
    g = n_heads // hb
    w = hb * HEAD_DIM
    return pl.pallas_call(
        functools.partial(_sb_attn_kernel, t=t, hb=hb, scale=HEAD_DIM ** -0.5),
        grid=(b, g, s // t),
        in_specs=[pl.BlockSpec((1, t, w), lambda bi, gi, i: (bi, i, gi)),
                  pl.BlockSpec((1, s, w), lambda bi, gi, i: (bi, 0, g + gi)),
                  pl.BlockSpec((1, s, w), lambda bi, gi, i: (bi, 0, 2 * g + gi))],
        out_specs=pl.BlockSpec((1, t, w), lambda bi, gi, i: (bi, i, gi)),
        out_shape=jax.ShapeDtypeStruct((b, s, n_heads * HEAD_DIM), BF16),
        compiler_params=_params("parallel", "parallel", "arbitrary"),
        name="sb_attention",
    )(qkv, qkv, qkv)


def _fox_gate_kernel(f_ref, bias_ref, col_ref, row_ref, carry_ref, *, t):
    i = pl.program_id(1)

    @pl.when(i == 0)
    def _():
        carry_ref[...] = jnp.zeros_like(carry_ref)

    x = f_ref[0] + bias_ref[...]
    log_f = -_softplus(-x)
    row = lax.broadcasted_iota(jnp.int32, (t, t), 0)
    col = lax.broadcasted_iota(jnp.int32, (t, t), 1)
    lower = jnp.where(col <= row, 1.0, 0.0).astype(BF16)
    hi, mid, lo = _split3(log_f)
    cum = _dot(lower, hi) + _dot(lower, mid) + _dot(lower, lo) + carry_ref[...]
    col_ref[0] = cum
    row_ref[0] = cum.T
    carry_ref[...] = cum[t - 1:t, :]


def fox_gate(f_raw, bias, t=512):
    b, s, n = f_raw.shape
    return pl.pallas_call(
        functools.partial(_fox_gate_kernel, t=t),
        grid=(b, s // t),
        in_specs=[pl.BlockSpec((1, t, n), lambda bi, i: (bi, i, 0)),
                  pl.BlockSpec((1, n), lambda bi, i: (0, 0))],
        out_specs=[pl.BlockSpec((1, t, n), lambda bi, i: (bi, i, 0)),
                   pl.BlockSpec((1, n, t), lambda bi, i: (bi, 0, i))],
        out_shape=[jax.ShapeDtypeStruct((b, s, n), F32), jax.ShapeDtypeStruct((b, n, s), F32)],
        scratch_shapes=[pltpu.VMEM((1, n), F32)],
        compiler_params=_params("parallel", "arbitrary"),
        name="fox_gate",
    )(f_raw, bias)


def _fox_attn_kernel(q_ref, k_ref, v_ref, gate_ref, cq_ref, ck_ref, o_ref, *, t, hb, scale):
    gi = pl.program_id(1)
    i = pl.program_id(2)
    q = _split_heads(q_ref[0], hb)
    lane = lax.broadcasted_iota(jnp.int32, (t, LANES), 1)
    cq = cq_ref[0]
    cum_q = jnp.stack([jnp.sum(jnp.where(lane == gi * hb + h, cq, 0.0), axis=1, keepdims=True)
                       for h in range(hb)], axis=0)
    row = lax.broadcasted_iota(jnp.int32, (t, t), 0)
    col = lax.broadcasted_iota(jnp.int32, (t, t), 1)
    causal = col <= row

    def block(kb, m, l, acc, masked):
        start = pl.multiple_of(kb * t, t)
        k = _split_heads(k_ref[0, pl.ds(start, t), :], hb)
        v = _split_heads(v_ref[0, pl.ds(start, t), :], hb)
        cum_k = ck_ref[0, :, pl.ds(kb, 1), :]
        logits = _bmm_nt(q, k) * scale + cum_q - cum_k
        if masked:
            logits = jnp.where(causal, logits, -jnp.inf)
        m_new = jnp.maximum(m, jnp.max(logits, axis=2, keepdims=True))
        alpha = jnp.exp(m - m_new)
        p = jnp.exp(logits - m_new)
        l = l * alpha + jnp.sum(p, axis=2, keepdims=True)
        acc = acc * alpha + _bmm(p.astype(BF16), v)
        return m_new, l, acc

    init = (jnp.full((hb, t, 1), -jnp.inf, F32), jnp.zeros((hb, t, 1), F32), jnp.zeros((hb, t, HEAD_DIM), F32))
    m, l, acc = block(i, *init, True)

    def body(n, c):
        return block(i - 1 - n, *c, False)

    m, l, acc = lax.fori_loop(0, i, body, (m, l, acc))
    out = acc / l
    for h in range(hb):
        hs = slice(h * HEAD_DIM, (h + 1) * HEAD_DIM)
        o_ref[0, :, hs] = (out[h] * _sigmoid(gate_ref[0, :, hs].astype(F32))).astype(o_ref.dtype)


def fox_attention(proj, cum_col, cum_row, n_heads, t=512, hb=2):
    b, s, _ = proj.shape
    hb = min(hb, n_heads)
    assert n_heads % hb == 0
    g = n_heads // hb
    w = hb * HEAD_DIM
    cum_row = cum_row.reshape(b, n_heads, s // t, t)
    return pl.pallas_call(
        functools.partial(_fox_attn_kernel, t=t, hb=hb, scale=HEAD_DIM ** -0.5),
        grid=(b, g, s // t),
        in_specs=[pl.BlockSpec((1, t, w), lambda bi, gi, i: (bi, i, gi)),
                  pl.BlockSpec((1, s, w), lambda bi, gi, i: (bi, 0, g + gi)),
                  pl.BlockSpec((1, s, w), lambda bi, gi, i: (bi, 0, 2 * g + gi)),
                  pl.BlockSpec((1, t, w), lambda bi, gi, i: (bi, i, 3 * g + gi)),
                  pl.BlockSpec((1, t, LANES), lambda bi, gi, i: (bi, i, 0)),
                  pl.BlockSpec((1, hb, s // t, t), lambda bi, gi, i: (bi, gi, 0, 0))],
        out_specs=pl.BlockSpec((1, t, w), lambda bi, gi, i: (bi, i, gi)),
        out_shape=jax.ShapeDtypeStruct((b, s, n_heads * HEAD_DIM), BF16),
        compiler_params=_params("parallel", "parallel", "arbitrary"),
        name="fox_attention",
    )(proj, proj, proj, proj, cum_col, cum_row)


def _gdn_gate_kernel(b_ref, a_ref, alog_ref, dt_ref, beta_ref, gc_ref, gl_ref, gct_ref, *, t, chunk):
    beta_ref[0] = _sigmoid(b_ref[0])
    g = -jnp.exp(alog_ref[...]) * _softplus(a_ref[0] + dt_ref[...])
    row = lax.broadcasted_iota(jnp.int32, (t, t), 0)
    col = lax.broadcasted_iota(jnp.int32, (t, t), 1)
    same = _block_id(row, chunk) == _block_id(col, chunk)
    lower = jnp.where(same & (col <= row), 1.0, 0.0).astype(BF16)
    whole = jnp.where(same, 1.0, 0.0).astype(BF16)
    hi, mid, lo = _split3(g)
    gc = _dot(lower, hi) + _dot(lower, mid) + _dot(lower, lo)
    gc_ref[0] = gc
    gl_ref[0] = _dot(whole, hi) + _dot(whole, mid) + _dot(whole, lo)
    gct_ref[0] = gc.T


def gdn_gates(b_raw, a_raw, a_log, dt_bias, t=512):
    b, s, n = b_raw.shape
    spec = pl.BlockSpec((1, t, n), lambda bi, i: (bi, i, 0))
    vec = pl.BlockSpec((1, n), lambda bi, i: (0, 0))
    return pl.pallas_call(
        functools.partial(_gdn_gate_kernel, t=t, chunk=GDN_CHUNK),
        grid=(b, s // t),
        in_specs=[spec, spec, vec, vec],
        out_specs=[spec, spec, spec, pl.BlockSpec((1, n, t), lambda bi, i: (bi, 0, i))],
        out_shape=[jax.ShapeDtypeStruct((b, s, n), F32)] * 3 + [jax.ShapeDtypeStruct((b, n, s), F32)],
        compiler_params=_params("parallel", "parallel"),
        name="gdn_gates",
    )(b_raw, a_raw, a_log, dt_bias)


def _gdn_prep_kernel(x_ref, prev_ref, w_ref, o_ref, *, ts, n_q_blocks, n_qk_blocks, heads_per_block):
    si = pl.program_id(1)
    ci = pl.program_id(2)
    cur = x_ref[0].astype(F32)
    prev = jnp.where(si == 0, 0.0, prev_ref[0].astype(F32))
    ext = jnp.concatenate([prev, cur], axis=0)
    w = w_ref[...]
    y = jnp.zeros_like(cur)
    for tap in range(GDN_CONV):
        off = 8 - (GDN_CONV - 1) + tap
        y = y + w[tap:tap + 1, :] * ext[off:off + ts, :]
    y = _silu(y)
    q_scale = jnp.where(ci < n_q_blocks, HEAD_DIM ** -0.5, 1.0)
    is_qk = ci < n_qk_blocks
    for hh in range(heads_per_block):
        sl = slice(hh * HEAD_DIM, (hh + 1) * HEAD_DIM)
        yh = y[:, sl]
        ss = jnp.sum(yh * yh, axis=1, keepdims=True)
        factor = jnp.where(is_qk, lax.rsqrt(ss + EPS) * q_scale, 1.0)
        o_ref[0, :, sl] = (yh * factor).astype(o_ref.dtype)


def gdn_prep(proj, conv_w_t, qkv_dim, k_dim, ts=512, tc=512):
    b, s, _ = proj.shape
    ts, tc = min(ts, s), min(tc, k_dim)
    rows8 = ts // 8
    return pl.pallas_call(
        functools.partial(_gdn_prep_kernel, ts=ts, n_q_blocks=k_dim // tc, n_qk_blocks=2 * k_dim // tc,
                          heads_per_block=tc // HEAD_DIM),
        grid=(b, s // ts, qkv_dim // tc),
        in_specs=[pl.BlockSpec((1, ts, tc), lambda bi, si, ci: (bi, si, ci)),
                  pl.BlockSpec((1, 8, tc), lambda bi, si, ci: (bi, jnp.maximum(si * rows8 - 1, 0), ci)),
                  pl.BlockSpec((GDN_CONV, tc), lambda bi, si, ci: (0, ci))],
        out_specs=pl.BlockSpec((1, ts, tc), lambda bi, si, ci: (bi, si, ci)),
        out_shape=jax.ShapeDtypeStruct((b, s, qkv_dim), BF16),
        compiler_params=_params("parallel", "parallel", "parallel"),
        name="gdn_prep",
    )(proj, proj, conv_w_t)


def _bmm(a, b):
    return lax.dot_general(a, b, (((2,), (1,)), ((0,), (0,))), preferred_element_type=F32)


def _bmm_nt(a, b):
    return lax.dot_general(a, b, (((2,), (2,)), ((0,), (0,))), preferred_element_type=F32)


def _bmm_tn(a, b):
    return lax.dot_general(a, b, (((1,), (1,)), ((0,), (0,))), preferred_element_type=F32)


def _inv_unit_lower(m_strict, row, col):
    c = m_strict.shape[-1]
    eye = jnp.where(row == col, 1.0, 0.0)
    n1 = jnp.where(_block_id(row, 8) == _block_id(col, 8), -m_strict, 0.0)
    n1b = n1.astype(BF16)
    n2 = _bmm(n1b, n1b)
    n2b = n2.astype(BF16)
    n4 = _bmm(n2b, n2b)
    x = _bmm((eye + n1).astype(BF16), (eye + n2).astype(BF16))
    x = _bmm(x.astype(BF16), (eye + n4).astype(BF16))
    size = 8
    while size < c:
        lower_left = ((_block_id(row, size) == _block_id(col, size) + 1)
                      & (_block_id(row, 2 * size) == _block_id(col, 2 * size)))
        cm = jnp.where(lower_left, m_strict, 0.0).astype(BF16)
        xb = x.astype(BF16)
        x = x - _bmm(_bmm(xb, cm).astype(BF16), xb)
        size *= 2
    return x


def _gdn_kernel(q_ref, k_ref, v_ref, z_ref, beta_ref, gc_ref, gl_ref, gct_ref, nw_ref, o_ref, state_ref,
                *, n_chunks):
    ki = pl.program_id(1)
    si = pl.program_id(2)
    c = GDN_CHUNK
    t = n_chunks * c

    @pl.when(si == 0)
    def _():
        state_ref[...] = jnp.zeros_like(state_ref)

    lane = lax.broadcasted_iota(jnp.int32, (t, LANES), 1)
    row = lax.broadcasted_iota(jnp.int32, (c, c), 0)
    col = lax.broadcasted_iota(jnp.int32, (c, c), 1)
    strict = col < row
    incl = col <= row

    nc = n_chunks

    def column(ref, head):
        col_vec = jnp.sum(jnp.where(lane == head, ref[0], 0.0), axis=1, keepdims=True)
        return col_vec.reshape(nc, c, 1)

    def both(per_head):
        return jnp.concatenate([per_head(0), per_head(1)], axis=0)

    beta = both(lambda hh: column(beta_ref, 2 * ki + hh))
    gcum = both(lambda hh: column(gc_ref, 2 * ki + hh))
    glast = both(lambda hh: column(gl_ref, 2 * ki + hh))
    g_row = both(lambda hh: gct_ref[0, 0, :, hh:hh + 1, :])
    q = q_ref[0].reshape(nc, c, HEAD_DIM)
    k = k_ref[0].reshape(nc, c, HEAD_DIM)
    kk = _bmm_nt(k, k)
    qk = _bmm_nt(q, k)
    qf = both(lambda hh: q.astype(F32))
    kf = both(lambda hh: k.astype(F32))
    v = both(lambda hh: v_ref[0, :, hh * HEAD_DIM:(hh + 1) * HEAD_DIM].reshape(nc, c, HEAD_DIM)).astype(F32)

    decay = jnp.exp(jnp.minimum(gcum - g_row, 0.0))
    m_strict = jnp.where(strict, both(lambda hh: kk) * beta * decay, 0.0)
    attn = jnp.where(incl, both(lambda hh: qk) * decay, 0.0).astype(BF16)
    t_inv = _inv_unit_lower(m_strict, row, col).astype(BF16)
    u = _bmm(t_inv, (v * beta).astype(BF16))
    w = _bmm(t_inv, (kf * (beta * jnp.exp(gcum))).astype(BF16)).astype(BF16)
    q_dec = (qf * jnp.exp(gcum)).astype(BF16)
    k_dec = (kf * jnp.exp(glast - gcum)).astype(BF16)
    chunk_decay = jnp.exp(glast[:, 0:1, :])
    nw = nw_ref[...]

    state = state_ref[...]
    for ch in range(nc):
        def pick(x):
            return jnp.concatenate([x[ch:ch + 1], x[nc + ch:nc + ch + 1]], axis=0)

        state_b = state.astype(BF16)
        v_new = pick(u) - _bmm(pick(w), state_b)
        v_new_b = v_new.astype(BF16)
        o = _bmm(pick(q_dec), state_b) + _bmm(pick(attn), v_new_b)
        state = state * pick(chunk_decay) + _bmm_tn(pick(k_dec), v_new_b)
        ms = jnp.mean(o * o, axis=2, keepdims=True)
        rows = slice(ch * c, (ch + 1) * c)
        zf = z_ref[0, rows, :].astype(F32)
        for hh in range(2):
            hs = slice(hh * HEAD_DIM, (hh + 1) * HEAD_DIM)
            o_ref[0, rows, hs] = (o[hh] * lax.rsqrt(ms[hh] + EPS) * nw * _silu(zf[:, hs])).astype(o_ref.dtype)
    state_ref[...] = state


def gdn_core(qkv_act, proj, beta, gc, gl, gct, norm_w, n_k_heads, n_chunks=4):
    b, s, _ = qkv_act.shape
    hk = n_k_heads
    n_chunks = min(n_chunks, s // GDN_CHUNK)
    t = n_chunks * GDN_CHUNK
    gate = pl.BlockSpec((1, t, LANES), lambda bi, ki, si: (bi, si, 0))
    return pl.pallas_call(
        functools.partial(_gdn_kernel, n_chunks=n_chunks),
        grid=(b, hk, s // t),
        in_specs=[pl.BlockSpec((1, t, HEAD_DIM), lambda bi, ki, si: (bi, si, ki)),
                  pl.BlockSpec((1, t, HEAD_DIM), lambda bi, ki, si: (bi, si, hk + ki)),
                  pl.BlockSpec((1, t, 2 * HEAD_DIM), lambda bi, ki, si: (bi, si, hk + ki)),
                  pl.BlockSpec((1, t, 2 * HEAD_DIM), lambda bi, ki, si: (bi, si, 2 * hk + ki)),
                  gate, gate, gate,
                  pl.BlockSpec((1, 1, n_chunks, 2, GDN_CHUNK), lambda bi, ki, si: (bi, ki, si, 0, 0)),
                  pl.BlockSpec((1, HEAD_DIM), lambda bi, ki, si: (0, 0))],
        out_specs=pl.BlockSpec((1, t, 2 * HEAD_DIM), lambda bi, ki, si: (bi, si, ki)),
        out_shape=jax.ShapeDtypeStruct((b, s, 2 * hk * HEAD_DIM), BF16),
        scratch_shapes=[pltpu.VMEM((2, HEAD_DIM, HEAD_DIM), F32)],
        compiler_params=_params("parallel", "parallel", "arbitrary"),
        name="gdn_core",
    )(qkv_act, qkv_act, qkv_act, proj, beta, gc, gl, gct, norm_w)


def _router_kernel(logit_ref, idx_ref, w_ref, *, n_experts):
    x = logit_ref[...]
    lane = lax.broadcasted_iota(jnp.int32, x.shape, 1)
    x = jnp.where(lane < n_experts, x, -jnp.inf)
    m1 = jnp.max(x, axis=1, keepdims=True)
    i1 = jnp.min(jnp.where(x == m1, lane, LANES), axis=1, keepdims=True)
    x2 = jnp.where(lane == i1, -jnp.inf, x)
    m2 = jnp.max(x2, axis=1, keepdims=True)
    i2 = jnp.min(jnp.where(x2 == m2, lane, LANES), axis=1, keepdims=True)
    e2 = jnp.exp(m2 - m1)
    w1 = 1.0 / (1.0 + e2)
    w2 = e2 / (1.0 + e2)
    idx_ref[...] = jnp.where(lane == 0, i1, jnp.where(lane == 1, i2, 0))
    w_ref[...] = jnp.where(lane == 0, w1, jnp.where(lane == 1, w2, 0.0))


def router_top2(logits, n_experts, tm=1024):
    m, n = logits.shape
    spec = pl.BlockSpec((tm, n), lambda i: (i, 0))
    return pl.pallas_call(
        functools.partial(_router_kernel, n_experts=n_experts),
        grid=(m // tm,),
        in_specs=[spec],
        out_specs=[spec, spec],
        out_shape=[jax.ShapeDtypeStruct((m, n), jnp.int32), jax.ShapeDtypeStruct((m, n), F32)],
        compiler_params=_params("parallel"),
        name="router",
    )(logits)


MOE_ROW_TILE = 256


def _route(idx, n_experts, tm):
    m = idx.shape[0]
    a = 2 * m
    e = idx.reshape(a)
    onehot = (e[:, None] == jnp.arange(n_experts, dtype=jnp.int32)[None, :]).astype(jnp.int32)
    csum = jnp.cumsum(onehot, axis=0)
    counts = csum[-1]
    padded = (counts + tm - 1) // tm * tm
    ends = jnp.cumsum(padded)
    pos = jnp.sum(onehot * (csum - 1 + (ends - padded)[None, :]), axis=1)
    n_tiles = a // tm + n_experts
    tile_start = jnp.arange(n_tiles, dtype=jnp.int32) * tm
    tile_expert = jnp.minimum(jnp.sum((tile_start[:, None] >= ends[None, :]).astype(jnp.int32), axis=1),
                              n_experts - 1)
    n_used = (ends[-1] // tm).astype(jnp.int32).reshape(1)
    row_token = jnp.zeros((n_tiles * tm,), jnp.int32).at[pos].set(jnp.arange(a, dtype=jnp.int32) // 2)
    return pos.astype(jnp.int32), tile_expert, n_used, row_token


DMA_ISSUE_UNROLL = 8


def _row_copy(src_hbm, src_row, dst_vmem, dst_row, sem):
    return pltpu.make_async_copy(src_hbm.at[pl.ds(src_row, 1)], dst_vmem.at[pl.ds(dst_row, 1)], sem)


def _rows_wait(src_hbm, dst_vmem, sem):
    pltpu.make_async_copy(src_hbm.at[pl.ds(0, dst_vmem.shape[0])], dst_vmem, sem).wait()


def _moe_gather_kernel(row_token_ref, n_used_ref, h_hbm, g_ref, o_ref, buf, sem, *, tm):
    i = pl.program_id(0)

    @pl.when(i < n_used_ref[0])
    def _():
        def start(r, c):
            _row_copy(h_hbm, row_token_ref[i * tm + r], buf, r, sem).start()
            return c

        lax.fori_loop(0, tm, start, 0, unroll=DMA_ISSUE_UNROLL)
        _rows_wait(h_hbm, buf, sem)
        x = buf[...]
        ms = jnp.mean(x * x, axis=-1, keepdims=True)
        o_ref[...] = (x * lax.rsqrt(ms + EPS) * g_ref[...]).astype(o_ref.dtype)

    @pl.when(i >= n_used_ref[0])
    def _():
        o_ref[...] = jnp.zeros_like(o_ref)


def moe_gather(h, g, row_token, n_used, tm):
    m, d = h.shape
    p = row_token.shape[0]
    return pl.pallas_call(
        functools.partial(_moe_gather_kernel, tm=tm),
        grid_spec=pltpu.PrefetchScalarGridSpec(
            num_scalar_prefetch=2,
            grid=(p // tm,),
            in_specs=[pl.BlockSpec(memory_space=pl.ANY),
                      pl.BlockSpec((1, d), lambda i, rt, nu: (0, 0))],
            out_specs=pl.BlockSpec((tm, d), lambda i, rt, nu: (i, 0)),
            scratch_shapes=[pltpu.VMEM((tm, d), F32), pltpu.SemaphoreType.DMA(())]),
        out_shape=jax.ShapeDtypeStruct((p, d), BF16),
        compiler_params=_params("arbitrary"),
        name="moe_gather",
    )(row_token, n_used, h, g.reshape(1, d))


def _moe_up_kernel(te_ref, n_used_ref, x_ref, wg_ref, wu_ref, o_ref):
    i = pl.program_id(1)

    @pl.when(i < n_used_ref[0])
    def _():
        x = x_ref[...]
        gate = _dot(x, wg_ref[...])
        up = _dot(x, wu_ref[...])
        o_ref[...] = (_silu(gate) * up).astype(o_ref.dtype)

    @pl.when(i >= n_used_ref[0])
    def _():
        o_ref[...] = jnp.zeros_like(o_ref)


def moe_up(xs, w_gate_up, layer, tile_expert, n_used, tm, tn=1408):
    p, d = xs.shape
    f = w_gate_up.shape[3] // 2
    tn = _tile(f, tn)
    nj = f // tn

    def row_tile(j, i, te, nu):
        return (jnp.minimum(i, nu[0] - 1), 0)

    return pl.pallas_call(
        _moe_up_kernel,
        grid_spec=pltpu.PrefetchScalarGridSpec(
            num_scalar_prefetch=2,
            grid=(nj, p // tm),
            in_specs=[pl.BlockSpec((tm, d), row_tile),
                      pl.BlockSpec((None, None, d, tn), lambda j, i, te, nu: (layer, te[i], 0, j)),
                      pl.BlockSpec((None, None, d, tn), lambda j, i, te, nu: (layer, te[i], 0, j + nj))],
            out_specs=pl.BlockSpec((tm, tn), lambda j, i, te, nu: (i, j))),
        out_shape=jax.ShapeDtypeStruct((p, f), BF16),
        compiler_params=_params("parallel", "arbitrary"),
        name="moe_up",
    )(tile_expert, n_used, xs, w_gate_up, w_gate_up)


def _moe_down_kernel(te_ref, n_used_ref, x_ref, w_ref, o_ref):
    i = pl.program_id(0)

    @pl.when(i < n_used_ref[0])
    def _():
        o_ref[...] = _dot(x_ref[...], w_ref[...])

    @pl.when(i >= n_used_ref[0])
    def _():
        o_ref[...] = jnp.zeros_like(o_ref)


def moe_down(hs, w_down, layer, tile_expert, n_used, tm):
    p, f = hs.shape
    d = w_down.shape[3]
    return pl.pallas_call(
        _moe_down_kernel,
        grid_spec=pltpu.PrefetchScalarGridSpec(
            num_scalar_prefetch=2,
            grid=(p // tm,),
            in_specs=[pl.BlockSpec((tm, f), lambda i, te, nu: (jnp.minimum(i, nu[0] - 1), 0)),
                      pl.BlockSpec((None, None, f, d), lambda i, te, nu: (layer, te[i], 0, 0))],
            out_specs=pl.BlockSpec((tm, d), lambda i, te, nu: (i, 0))),
        out_shape=jax.ShapeDtypeStruct((p, d), F32),
        compiler_params=_params("arbitrary"),
        name="moe_down",
    )(tile_expert, n_used, hs, w_down)


def _moe_combine_kernel(pos_ref, h_ref, w_ref, y_hbm, o_ref, buf, sem, *, tm):
    i = pl.program_id(0)

    def start(r, c):
        for k in range(2):
            _row_copy(y_hbm, pos_ref[2 * (i * tm + r) + k], buf.at[k], r, sem).start()
        return c

    lax.fori_loop(0, tm, start, 0, unroll=DMA_ISSUE_UNROLL)
    for k in range(2):
        _rows_wait(y_hbm, buf.at[k], sem)
    w = w_ref[...]
    o_ref[...] = h_ref[...] + w[:, 0:1] * buf[0] + w[:, 1:2] * buf[1]


def moe_combine(h, top_w, ys, pos, tm=256):
    m, d = h.shape
    return pl.pallas_call(
        functools.partial(_moe_combine_kernel, tm=tm),
        grid_spec=pltpu.PrefetchScalarGridSpec(
            num_scalar_prefetch=1,
            grid=(m // tm,),
            in_specs=[pl.BlockSpec((tm, d), lambda i, pos: (i, 0)),
                      pl.BlockSpec((tm, LANES), lambda i, pos: (i, 0)),
                      pl.BlockSpec(memory_space=pl.ANY)],
            out_specs=pl.BlockSpec((tm, d), lambda i, pos: (i, 0)),
            scratch_shapes=[pltpu.VMEM((2, tm, d), F32), pltpu.SemaphoreType.DMA(())]),
        out_shape=jax.ShapeDtypeStruct((m, d), F32),
        compiler_params=_params("arbitrary"),
        name="moe_combine",
    )(pos, h, top_w, ys)


def _sb_layer(h, g, w_in, w_out, layer):
    b, s, d = h.shape
    n_heads = w_out.shape[1] // HEAD_DIM
    hn = rmsnorm(h.reshape(b * s, d), g, BF16)
    qkv = matmul(hn, w_in, layer, w_in.shape[2], BF16)
    o = sb_attention(qkv.reshape(b, s, -1), n_heads)
    return matmul(o.reshape(b * s, -1), w_out, layer, d, F32, res=h.reshape(b * s, d)).reshape(b, s, d)


def _fox_layer(h, g, w_in, b_f, w_out, layer):
    b, s, d = h.shape
    n_heads = w_out.shape[1] // HEAD_DIM
    main = 4 * n_heads * HEAD_DIM
    h2 = h.reshape(b * s, d)
    hn = rmsnorm(h2, g, BF16)
    proj = matmul(hn, w_in, layer, main, BF16).reshape(b, s, main)
    f_raw = small_proj(h2, g, _pad_cols(w_in[layer, :, main:], LANES)).reshape(b, s, LANES)
    bias = jnp.pad(b_f, (0, LANES - n_heads)).reshape(1, LANES)
    cum_col, cum_t = fox_gate(f_raw, bias)
    o = fox_attention(proj, cum_col, cum_t[:, :n_heads, :], n_heads)
    return matmul(o.reshape(b * s, -1), w_out, layer, d, F32, res=h2).reshape(b, s, d)


def _gdn_layer(h, g, w_in, conv_w, a_log, dt_bias, norm_w, w_out, layer):
    b, s, d = h.shape
    v_dim = w_out.shape[1]
    n_v_heads = v_dim // HEAD_DIM
    n_k_heads = n_v_heads // 2
    k_dim = n_k_heads * HEAD_DIM
    qkv_dim = 2 * k_dim + v_dim
    main = qkv_dim + v_dim
    h2 = h.reshape(b * s, d)
    hn = rmsnorm(h2, g, BF16)
    proj = matmul(hn, w_in, layer, main, BF16).reshape(b, s, main)
    w_gates = jnp.concatenate([_pad_cols(w_in[layer, :, main:main + n_v_heads], LANES),
                               _pad_cols(w_in[layer, :, main + n_v_heads:], LANES)], axis=1)
    raw = small_proj(h2, g, w_gates).reshape(b, s, 2 * LANES)
    pad = (0, LANES - n_v_heads)
    beta, gc, gl, gct = gdn_gates(raw[..., :LANES], raw[..., LANES:],
                                  jnp.pad(a_log, pad).reshape(1, LANES), jnp.pad(dt_bias, pad).reshape(1, LANES))
    gct = gct[:, :n_v_heads, :].reshape(b, n_k_heads, 2, s // GDN_CHUNK, GDN_CHUNK).transpose(0, 1, 3, 2, 4)
    qkv_act = gdn_prep(proj, conv_w.T, qkv_dim, k_dim)
    o = gdn_core(qkv_act, proj, beta, gc, gl, gct, norm_w.reshape(1, HEAD_DIM), n_k_heads)
    return matmul(o.reshape(b * s, v_dim), w_out, layer, d, F32, res=h2).reshape(b, s, d)


def _dense_ffn(h, g, w_gate_up, w_down, layer):
    b, s, d = h.shape
    h2 = h.reshape(b * s, d)
    hn = rmsnorm(h2, g, BF16)
    mid = swiglu_up(hn, w_gate_up, layer)
    return matmul(mid, w_down, layer, d, F32, res=h2).reshape(b, s, d)


def _moe_ffn(h, g, w_router, w_gate_up, w_down, layer):
    b, s, d = h.shape
    n_experts = w_router.shape[1]
    h2 = h.reshape(b * s, d)
    logits = small_proj(h2, g, _pad_cols(w_router, LANES))
    top_idx, top_w = router_top2(logits, n_experts)
    tm = MOE_ROW_TILE
    pos, tile_expert, n_used, row_token = _route(top_idx[:, :2], n_experts, tm)
    xs = moe_gather(h2, g, row_token, n_used, tm)
    mid = moe_up(xs, w_gate_up, layer, tile_expert, n_used, tm)
    ys = moe_down(mid, w_down, layer, tile_expert, n_used, tm)
    return moe_combine(h2, top_w, ys, pos).reshape(b, s, d)


def kernel(x, norm_mix, norm_ffn, sb_w_in, sb_w_out, gdn_w_in, gdn_conv_w, gdn_a_log, gdn_dt_bias, gdn_norm_w,
           gdn_w_out, fox_w_in, fox_b_f, fox_w_out, ffn_w_gate_up, ffn_w_down, moe_w_router, moe_w_gate_up,
           moe_w_down, final_norm):
    depth = norm_mix.shape[0]
    moe_gate_up_b = moe_w_gate_up.astype(BF16)
    moe_down_b = moe_w_down.astype(BF16)
    h = x
    for i in range(depth):
        kind, j = i % 3, i // 3
        if kind == 0:
            h = _sb_layer(h, norm_mix[i], sb_w_in, sb_w_out, j)
        elif kind == 1:
            h = _gdn_layer(h, norm_mix[i], gdn_w_in, gdn_conv_w[j], gdn_a_log[j], gdn_dt_bias[j],
                           gdn_norm_w[j], gdn_w_out, j)
        else:
            h = _fox_layer(h, norm_mix[i], fox_w_in, fox_b_f[j], fox_w_out, j)
        f = i // 2
        if i % 2 == 0:
            h = _dense_ffn(h, norm_ffn[i], ffn_w_gate_up, ffn_w_down, f)
        else:
            h = _moe_ffn(h, norm_ffn[i], moe_w_router[f], moe_gate_up_b, moe_down_b, f)
    b, s, d = h.shape
    return rmsnorm(h.reshape(b * s, d), final_norm, x.dtype).reshape(b, s, d)
```

```python
import functools

import jax
import jax.numpy as jnp
from jax import lax
from jax.experimental import pallas as pl
from jax.experimental.pallas import tpu as pltpu

HEAD_DIM = 128
LANES = 128
GDN_CONV = 4
GDN_CHUNK = 128
EPS = 1e-6
LOG2E = 1.4426950408889634
EXP_UNDERFLOW = -104.0
VMEM_LIMIT_BYTES = 56 * 1024 * 1024

F32 = jnp.float32
BF16 = jnp.bfloat16


def _params(*semantics):
    return pltpu.CompilerParams(dimension_semantics=semantics, vmem_limit_bytes=VMEM_LIMIT_BYTES)


def _tile(dim, preferred):
    assert dim % LANES == 0, dim
    t = min(preferred, dim) // LANES * LANES
    while dim % t:
        t -= LANES
    return t


def _dot(a, b):
    return jnp.dot(a, b, preferred_element_type=F32)


def _split3(x):
    hi = x.astype(BF16)
    r = x - hi.astype(F32)
    mid = r.astype(BF16)
    lo = (r - mid.astype(F32)).astype(BF16)
    return hi, mid, lo


def _split2(x):
    hi = x.astype(BF16)
    lo = (x - hi.astype(F32)).astype(BF16)
    return hi, lo


def _softplus(z):
    return jnp.maximum(z, 0.0) + jnp.log(1.0 + jnp.exp(-jnp.abs(z)))


def _sigmoid(z):
    return 1.0 / (1.0 + jnp.exp(-z))


def _silu(z):
    return z * _sigmoid(z)


def _block_id(idx, size):
    shift = size.bit_length() - 1
    assert 1 << shift == size
    return jnp.right_shift(idx, shift)


def _rmsnorm_kernel(x_ref, g_ref, o_ref):
    x = x_ref[...]
    ms = jnp.mean(x * x, axis=-1, keepdims=True)
    o_ref[...] = (x * lax.rsqrt(ms + EPS) * g_ref[...]).astype(o_ref.dtype)


def rmsnorm(x, g, out_dtype, tm=512):
    m, d = x.shape
    return pl.pallas_call(
        _rmsnorm_kernel,
        grid=(m // tm,),
        in_specs=[pl.BlockSpec((tm, d), lambda i: (i, 0)),
                  pl.BlockSpec((1, d), lambda i: (0, 0))],
        out_specs=pl.BlockSpec((tm, d), lambda i: (i, 0)),
        out_shape=jax.ShapeDtypeStruct((m, d), out_dtype),
        compiler_params=_params("parallel"),
        name="rmsnorm",
    )(x, g.reshape(1, d))


WS_VMEM_BUDGET = 44 * 1024 * 1024


def _ws_tiles(m, kdim, n, n_weight_blocks, out_bytes, has_res):
    for tm_pref, tn_pref in ((1024, 1024), (1024, 512), (512, 512), (512, 256), (256, 256), (256, 128)):
        tm, tn = _tile(m, tm_pref), _tile(n, tn_pref)
        weights = n_weight_blocks * kdim * tn * (2 * 4 + 2)
        acts = 2 * tm * kdim * 2
        outs = 2 * tm * tn * (out_bytes + (4 if has_res else 0))
        results = n_weight_blocks * tm * tn * 4
        if weights + acts + outs + results <= WS_VMEM_BUDGET:
            return tm, tn
    raise ValueError(f"no weight-stationary tiling fits VMEM for {(m, kdim, n)}")


def _matmul_kernel(*refs, has_res):
    if has_res:
        a_ref, w_ref, res_ref, o_ref, wb_ref = refs
    else:
        a_ref, w_ref, o_ref, wb_ref = refs

    @pl.when(pl.program_id(1) == 0)
    def _():
        wb_ref[...] = w_ref[...].astype(BF16)

    r = _dot(a_ref[...], wb_ref[...])
    if has_res:
        r = r + res_ref[...]
    o_ref[...] = r.astype(o_ref.dtype)


def matmul(a, w, layer, n_cols, out_dtype, res=None):
    m, kdim = a.shape
    tm, tn = _ws_tiles(m, kdim, n_cols, 1, jnp.dtype(out_dtype).itemsize, res is not None)
    in_specs = [pl.BlockSpec((tm, kdim), lambda j, i: (i, 0)),
                pl.BlockSpec((None, kdim, tn), lambda j, i: (layer, 0, j))]
    args = [a, w]
    if res is not None:
        in_specs.append(pl.BlockSpec((tm, tn), lambda j, i: (i, j)))
        args.append(res)
    return pl.pallas_call(
        functools.partial(_matmul_kernel, has_res=res is not None),
        grid=(n_cols // tn, m // tm),
        in_specs=in_specs,
        out_specs=pl.BlockSpec((tm, tn), lambda j, i: (i, j)),
        out_shape=jax.ShapeDtypeStruct((m, n_cols), out_dtype),
        scratch_shapes=[pltpu.VMEM((kdim, tn), BF16)],
        compiler_params=_params("parallel", "arbitrary"),
        name="matmul",
    )(*args)


def _swiglu_up_kernel(a_ref, wg_ref, wu_ref, o_ref, wgb_ref, wub_ref):
    @pl.when(pl.program_id(1) == 0)
    def _():
        wgb_ref[...] = wg_ref[...].astype(BF16)
        wub_ref[...] = wu_ref[...].astype(BF16)

    a = a_ref[...]
    o_ref[...] = (_silu(_dot(a, wgb_ref[...])) * _dot(a, wub_ref[...])).astype(o_ref.dtype)


def swiglu_up(a, w_gate_up, layer):
    m, kdim = a.shape
    f = w_gate_up.shape[2] // 2
    tm, tn = _ws_tiles(m, kdim, f, 2, 2, False)
    nj = f // tn
    return pl.pallas_call(
        _swiglu_up_kernel,
        grid=(nj, m // tm),
        in_specs=[pl.BlockSpec((tm, kdim), lambda j, i: (i, 0)),
                  pl.BlockSpec((None, kdim, tn), lambda j, i: (layer, 0, j)),
                  pl.BlockSpec((None, kdim, tn), lambda j, i: (layer, 0, j + nj))],
        out_specs=pl.BlockSpec((tm, tn), lambda j, i: (i, j)),
        out_shape=jax.ShapeDtypeStruct((m, f), BF16),
        scratch_shapes=[pltpu.VMEM((kdim, tn), BF16), pltpu.VMEM((kdim, tn), BF16)],
        compiler_params=_params("parallel", "arbitrary"),
        name="swiglu_up",
    )(a, w_gate_up, w_gate_up)


def _small_proj_kernel(h_ref, g_ref, w_ref, o_ref):
    x = h_ref[...]
    ms = jnp.mean(x * x, axis=-1, keepdims=True)
    xn = x * lax.rsqrt(ms + EPS) * g_ref[...]
    xh, xl = _split2(xn)
    wh, wl = _split2(w_ref[...])
    o_ref[...] = _dot(xh, wh) + _dot(xl, wh) + _dot(xh, wl)


def small_proj(h, g, w, tm=512):
    m, d = h.shape
    n = w.shape[1]
    return pl.pallas_call(
        _small_proj_kernel,
        grid=(m // tm,),
        in_specs=[pl.BlockSpec((tm, d), lambda i: (i, 0)),
                  pl.BlockSpec((1, d), lambda i: (0, 0)),
                  pl.BlockSpec((d, n), lambda i: (0, 0))],
        out_specs=pl.BlockSpec((tm, n), lambda i: (i, 0)),
        out_shape=jax.ShapeDtypeStruct((m, n), F32),
        compiler_params=_params("parallel"),
        name="small_proj",
    )(h, g.reshape(1, d), w)


def _pad_cols(w, width):
    return jnp.pad(w, ((0, 0), (0, width - w.shape[1])))


def _split_heads(x, n):
    return jnp.stack([x[:, h * HEAD_DIM:(h + 1) * HEAD_DIM] for h in range(n)], axis=0)


def _sb_attn_kernel(q_ref, k_ref, v_ref, o_ref, *, t, hb, scale):
    i = pl.program_id(2)
    q = _split_heads(q_ref[0], hb)
    row = lax.broadcasted_iota(jnp.int32, (t, t), 0)
    col = lax.broadcasted_iota(jnp.int32, (t, t), 1)
    strict = col < row
    suffix = jnp.where(row > col, 1.0, 0.0).astype(BF16)

    def block(kb, carry, acc, masked):
        start = pl.multiple_of(kb * t, t)
        k = _split_heads(k_ref[0, pl.ds(start, t), :], hb)
        v = _split_heads(v_ref[0, pl.ds(start, t), :], hb)
        z = _bmm_nt(q, k) * scale
        sp = _softplus(z)
        log_keep = -sp
        if masked:
            log_keep = jnp.where(strict, log_keep, 0.0)
        hi, lo = _split2(log_keep.reshape(hb * t, t))
        later = (_dot(hi, suffix) + _dot(lo, suffix)).reshape(hb, t, t) + carry
        w = jnp.exp(z - sp + later)
        if masked:
            w = jnp.where(strict, w, 0.0)
        acc = acc + _bmm(w.astype(BF16), v)
        carry = carry + jnp.sum(log_keep, axis=2, keepdims=True)
        return carry, acc

    carry, acc = block(i, jnp.zeros((hb, t, 1), F32), jnp.zeros((hb, t, HEAD_DIM), F32), True)

    def live(c):
        return (c[0] < i) & (c[3] > EXP_UNDERFLOW)

    def body(c):
        carry, acc = block(i - 1 - c[0], c[1], c[2], False)
        return c[0] + 1, carry, acc, jnp.max(carry)

    _, carry, acc, _ = lax.while_loop(live, body, (jnp.int32(0), carry, acc, jnp.max(carry)))
    for h in range(hb):
        o_ref[0, :, h * HEAD_DIM:(h + 1) * HEAD_DIM] = acc[h].astype(o_ref.dtype)


def sb_attention(qkv, n_heads, t=256, hb=4):
    b, s, _ = qkv.shape
    hb = min(hb, n_heads)
    assert n_heads % hb == 0
    g = n_heads // hb
    w = hb * HEAD_DIM
    return pl.pallas_call(
        functools.partial(_sb_attn_kernel, t=t, hb=hb, scale=HEAD_DIM ** -0.5),
        grid=(b, g, s // t),
        in_specs=[pl.BlockSpec((1, t, w), lambda bi, gi, i: (bi, i, gi)),
                  pl.BlockSpec((1, s, w), lambda bi, gi, i: (bi, 0, g + gi)),
                  pl.BlockSpec((1, s, w), lambda bi, gi, i: (bi, 0, 2 * g + gi))],
        out_specs=pl.BlockSpec((1, t, w), lambda bi, gi, i: (bi, i, gi)),
        out_shape=jax.ShapeDtypeStruct((b, s, n_heads * HEAD_DIM), BF16),
        compiler_params=_params("parallel", "parallel", "arbitrary"),
        name="sb_attention",
    )(qkv, qkv, qkv)


def _fox_gate_kernel(f_ref, bias_ref, col_ref, row_ref, carry_ref, *, t):
    i = pl.program_id(1)

    @pl.when(i == 0)
    def _():
        carry_ref[...] = jnp.zeros_like(carry_ref)

    x = f_ref[0] + bias_ref[...]
    log_f = -_softplus(-x)
    row = lax.broadcasted_iota(jnp.int32, (t, t), 0)
    col = lax.broadcasted_iota(jnp.int32, (t, t), 1)
    lower = jnp.where(col <= row, 1.0, 0.0).astype(BF16)
    hi, mid, lo = _split3(log_f)
    cum = _dot(lower, hi) + _dot(lower, mid) + _dot(lower, lo) + carry_ref[...]
    col_ref[0] = cum
    row_ref[0] = cum.T
    carry_ref[...] = cum[t - 1:t, :]


def fox_gate(f_raw, bias, t=512):
    b, s, n = f_raw.shape
    return pl.pallas_call(
        functools.partial(_fox_gate_kernel, t=t),
        grid=(b, s // t),
        in_specs=[pl.BlockSpec((1, t, n), lambda bi, i: (bi, i, 0)),
                  pl.BlockSpec((1, n), lambda bi, i: (0, 0))],
        out_specs=[pl.BlockSpec((1, t, n), lambda bi, i: (bi, i, 0)),
                   pl.BlockSpec((1, n, t), lambda bi, i: (bi, 0, i))],
        out_shape=[jax.ShapeDtypeStruct((b, s, n), F32), jax.ShapeDtypeStruct((b, n, s), F32)],
        scratch_shapes=[pltpu.VMEM((1, n), F32)],
        compiler_params=_params("parallel", "arbitrary"),
        name="fox_gate",
    )(f_raw, bias)


def _fox_attn_kernel(q_ref, k_ref, v_ref, gate_ref, cq_ref, ck_ref, o_ref, *, t, hb, scale):
    gi = pl.program_id(1)
    i = pl.program_id(2)
    q = _split_heads(q_ref[0], hb)
    lane = lax.broadcasted_iota(jnp.int32, (t, LANES), 1)
    cq = cq_ref[0]
    cum_q = jnp.stack([jnp.sum(jnp.where(lane == gi * hb + h, cq, 0.0), axis=1, keepdims=True)
                       for h in range(hb)], axis=0) * LOG2E
    row = lax.broadcasted_iota(jnp.int32, (t, t), 0)
    col = lax.broadcasted_iota(jnp.int32, (t, t), 1)
    causal = col <= row

    def block(kb, m, l, acc, masked):
        start = pl.multiple_of(kb * t, t)
        k = _split_heads(k_ref[0, pl.ds(start, t), :], hb)
        v = _split_heads(v_ref[0, pl.ds(start, t), :], hb)
        cum_k = ck_ref[0, :, pl.ds(kb, 1), :] * LOG2E
        logits = _bmm_nt(q, k) * (scale * LOG2E) + cum_q - cum_k
        if masked:
            logits = jnp.where(causal, logits, -jnp.inf)
        m_new = jnp.maximum(m, jnp.max(logits, axis=2, keepdims=True))
        alpha = jnp.exp2(m - m_new)
        p = jnp.exp2(logits - m_new)
        l = l * alpha + jnp.sum(p, axis=2, keepdims=True)
        acc = acc * alpha + _bmm(p.astype(BF16), v)
        return m_new, l, acc

    init = (jnp.full((hb, t, 1), -jnp.inf, F32), jnp.zeros((hb, t, 1), F32), jnp.zeros((hb, t, HEAD_DIM), F32))
    m, l, acc = block(i, *init, True)

    def body(n, c):
        return block(i - 1 - n, *c, False)

    m, l, acc = lax.fori_loop(0, i, body, (m, l, acc))
    out = acc / l
    for h in range(hb):
        hs = slice(h * HEAD_DIM, (h + 1) * HEAD_DIM)
        o_ref[0, :, hs] = (out[h] * _sigmoid(gate_ref[0, :, hs].astype(F32))).astype(o_ref.dtype)


def fox_attention(proj, cum_col, cum_row, n_heads, t=512, hb=2):
    b, s, _ = proj.shape
    hb = min(hb, n_heads)
    assert n_heads % hb == 0
    g = n_heads // hb
    w = hb * HEAD_DIM
    cum_row = cum_row.reshape(b, n_heads, s // t, t)
    return pl.pallas_call(
        functools.partial(_fox_attn_kernel, t=t, hb=hb, scale=HEAD_DIM ** -0.5),
        grid=(b, g, s // t),
        in_specs=[pl.BlockSpec((1, t, w), lambda bi, gi, i: (bi, i, gi)),
                  pl.BlockSpec((1, s, w), lambda bi, gi, i: (bi, 0, g + gi)),
                  pl.BlockSpec((1, s, w), lambda bi, gi, i: (bi, 0, 2 * g + gi)),
                  pl.BlockSpec((1, t, w), lambda bi, gi, i: (bi, i, 3 * g + gi)),
                  pl.BlockSpec((1, t, LANES), lambda bi, gi, i: (bi, i, 0)),
                  pl.BlockSpec((1, hb, s // t, t), lambda bi, gi, i: (bi, gi, 0, 0))],
        out_specs=pl.BlockSpec((1, t, w), lambda bi, gi, i: (bi, i, gi)),
        out_shape=jax.ShapeDtypeStruct((b, s, n_heads * HEAD_DIM), BF16),
        compiler_params=_params("parallel", "parallel", "arbitrary"),
        name="fox_attention",
    )(proj, proj, proj, proj, cum_col, cum_row)


def _gdn_gate_kernel(b_ref, a_ref, alog_ref, dt_ref, beta_ref, gc_ref, gl_ref, gct_ref, *, t, chunk):
    beta_ref[0] = _sigmoid(b_ref[0])
    g = -jnp.exp(alog_ref[...]) * _softplus(a_ref[0] + dt_ref[...])
    row = lax.broadcasted_iota(jnp.int32, (t, t), 0)
    col = lax.broadcasted_iota(jnp.int32, (t, t), 1)
    same = _block_id(row, chunk) == _block_id(col, chunk)
    lower = jnp.where(same & (col <= row), 1.0, 0.0).astype(BF16)
    whole = jnp.where(same, 1.0, 0.0).astype(BF16)
    hi, mid, lo = _split3(g)
    gc = _dot(lower, hi) + _dot(lower, mid) + _dot(lower, lo)
    gc_ref[0] = gc
    gl_ref[0] = _dot(whole, hi) + _dot(whole, mid) + _dot(whole, lo)
    gct_ref[0] = gc.T


def gdn_gates(b_raw, a_raw, a_log, dt_bias, t=512):
    b, s, n = b_raw.shape
    spec = pl.BlockSpec((1, t, n), lambda bi, i: (bi, i, 0))
    vec = pl.BlockSpec((1, n), lambda bi, i: (0, 0))
    return pl.pallas_call(
        functools.partial(_gdn_gate_kernel, t=t, chunk=GDN_CHUNK),
        grid=(b, s // t),
        in_specs=[spec, spec, vec, vec],
        out_specs=[spec, spec, spec, pl.BlockSpec((1, n, t), lambda bi, i: (bi, 0, i))],
        out_shape=[jax.ShapeDtypeStruct((b, s, n), F32)] * 3 + [jax.ShapeDtypeStruct((b, n, s), F32)],
        compiler_params=_params("parallel", "parallel"),
        name="gdn_gates",
    )(b_raw, a_raw, a_log, dt_bias)


def _gdn_prep_kernel(x_ref, prev_ref, w_ref, o_ref, *, ts, n_q_blocks, n_qk_blocks, heads_per_block):
    si = pl.program_id(1)
    ci = pl.program_id(2)
    cur = x_ref[0].astype(F32)
    prev = jnp.where(si == 0, 0.0, prev_ref[0].astype(F32))
    ext = jnp.concatenate([prev, cur], axis=0)
    w = w_ref[...]
    y = jnp.zeros_like(cur)
    for tap in range(GDN_CONV):
        off = 8 - (GDN_CONV - 1) + tap
        y = y + w[tap:tap + 1, :] * ext[off:off + ts, :]
    y = _silu(y)
    q_scale = jnp.where(ci < n_q_blocks, HEAD_DIM ** -0.5, 1.0)
    is_qk = ci < n_qk_blocks
    for hh in range(heads_per_block):
        sl = slice(hh * HEAD_DIM, (hh + 1) * HEAD_DIM)
        yh = y[:, sl]
        ss = jnp.sum(yh * yh, axis=1, keepdims=True)
        factor = jnp.where(is_qk, lax.rsqrt(ss + EPS) * q_scale, 1.0)
        o_ref[0, :, sl] = (yh * factor).astype(o_ref.dtype)


def gdn_prep(proj, conv_w_t, qkv_dim, k_dim, ts=512, tc=512):
    b, s, _ = proj.shape
    ts, tc = min(ts, s), min(tc, k_dim)
    rows8 = ts // 8
    return pl.pallas_call(
        functools.partial(_gdn_prep_kernel, ts=ts, n_q_blocks=k_dim // tc, n_qk_blocks=2 * k_dim // tc,
                          heads_per_block=tc // HEAD_DIM),
        grid=(b, s // ts, qkv_dim // tc),
        in_specs=[pl.BlockSpec((1, ts, tc), lambda bi, si, ci: (bi, si, ci)),
                  pl.BlockSpec((1, 8, tc), lambda bi, si, ci: (bi, jnp.maximum(si * rows8 - 1, 0), ci)),
                  pl.BlockSpec((GDN_CONV, tc), lambda bi, si, ci: (0, ci))],
        out_specs=pl.BlockSpec((1, ts, tc), lambda bi, si, ci: (bi, si, ci)),
        out_shape=jax.ShapeDtypeStruct((b, s, qkv_dim), BF16),
        compiler_params=_params("parallel", "parallel", "parallel"),
        name="gdn_prep",
    )(proj, proj, conv_w_t)


def _bmm(a, b):
    return lax.dot_general(a, b, (((2,), (1,)), ((0,), (0,))), preferred_element_type=F32)


def _bmm_nt(a, b):
    return lax.dot_general(a, b, (((2,), (2,)), ((0,), (0,))), preferred_element_type=F32)


def _bmm_tn(a, b):
    return lax.dot_general(a, b, (((1,), (1,)), ((0,), (0,))), preferred_element_type=F32)


def _inv_unit_lower(m_strict, row, col):
    c = m_strict.shape[-1]
    eye = jnp.where(row == col, 1.0, 0.0)
    n1 = jnp.where(_block_id(row, 8) == _block_id(col, 8), -m_strict, 0.0)
    n1b = n1.astype(BF16)
    n2 = _bmm(n1b, n1b)
    n2b = n2.astype(BF16)
    n4 = _bmm(n2b, n2b)
    x = _bmm((eye + n1).astype(BF16), (eye + n2).astype(BF16))
    x = _bmm(x.astype(BF16), (eye + n4).astype(BF16))
    size = 8
    while size < c:
        lower_left = ((_block_id(row, size) == _block_id(col, size) + 1)
                      & (_block_id(row, 2 * size) == _block_id(col, 2 * size)))
        cm = jnp.where(lower_left, m_strict, 0.0).astype(BF16)
        xb = x.astype(BF16)
        x = x - _bmm(_bmm(xb, cm).astype(BF16), xb)
        size *= 2
    return x


def _gdn_kernel(q_ref, k_ref, v_ref, z_ref, beta_ref, gc_ref, gl_ref, gct_ref, nw_ref, o_ref, state_ref,
                *, n_chunks, nkh):
    gi = pl.program_id(1)
    si = pl.program_id(2)
    c = GDN_CHUNK
    t = n_chunks * c
    nvh = 2 * nkh

    @pl.when(si == 0)
    def _():
        state_ref[...] = jnp.zeros_like(state_ref)

    lane = lax.broadcasted_iota(jnp.int32, (t, LANES), 1)
    row = lax.broadcasted_iota(jnp.int32, (c, c), 0)
    col = lax.broadcasted_iota(jnp.int32, (c, c), 1)
    strict = col < row
    incl = col <= row

    nc = n_chunks

    def column(ref, head):
        col_vec = jnp.sum(jnp.where(lane == head, ref[0], 0.0), axis=1, keepdims=True)
        return col_vec.reshape(nc, c, 1)

    def heads(per_head):
        return jnp.concatenate([per_head(vh) for vh in range(nvh)], axis=0)

    def head_cols(ref, vh):
        return ref[0, :, vh * HEAD_DIM:(vh + 1) * HEAD_DIM].reshape(nc, c, HEAD_DIM)

    beta = heads(lambda vh: column(beta_ref, gi * nvh + vh))
    gcum = heads(lambda vh: column(gc_ref, gi * nvh + vh))
    glast = heads(lambda vh: column(gl_ref, gi * nvh + vh))
    g_row = heads(lambda vh: gct_ref[0, vh // 2, :, vh % 2:vh % 2 + 1, :])
    q = [head_cols(q_ref, kh) for kh in range(nkh)]
    k = [head_cols(k_ref, kh) for kh in range(nkh)]
    kk = [_bmm_nt(k[kh], k[kh]) for kh in range(nkh)]
    qk = [_bmm_nt(q[kh], k[kh]) for kh in range(nkh)]
    qf = heads(lambda vh: q[vh // 2].astype(F32))
    kf = heads(lambda vh: k[vh // 2].astype(F32))
    v = heads(lambda vh: head_cols(v_ref, vh)).astype(F32)

    decay = jnp.exp(jnp.minimum(gcum - g_row, 0.0))
    m_strict = jnp.where(strict, heads(lambda vh: kk[vh // 2]) * beta * decay, 0.0)
    attn = jnp.where(incl, heads(lambda vh: qk[vh // 2]) * decay, 0.0).astype(BF16)
    t_inv = _inv_unit_lower(m_strict, row, col).astype(BF16)
    u = _bmm(t_inv, (v * beta).astype(BF16))
    w = _bmm(t_inv, (kf * (beta * jnp.exp(gcum))).astype(BF16)).astype(BF16)
    q_dec = (qf * jnp.exp(gcum)).astype(BF16)
    k_dec = (kf * jnp.exp(glast - gcum)).astype(BF16)
    chunk_decay = jnp.exp(glast[:, 0:1, :])
    nw = nw_ref[...]

    state = state_ref[...]
    for ch in range(nc):
        def pick(x):
            return jnp.concatenate([x[vh * nc + ch:vh * nc + ch + 1] for vh in range(nvh)], axis=0)

        state_b = state.astype(BF16)
        v_new = pick(u) - _bmm(pick(w), state_b)
        v_new_b = v_new.astype(BF16)
        o = _bmm(pick(q_dec), state_b) + _bmm(pick(attn), v_new_b)
        state = state * pick(chunk_decay) + _bmm_tn(pick(k_dec), v_new_b)
        ms = jnp.mean(o * o, axis=2, keepdims=True)
        rows = slice(ch * c, (ch + 1) * c)
        zf = z_ref[0, rows, :].astype(F32)
        for vh in range(nvh):
            hs = slice(vh * HEAD_DIM, (vh + 1) * HEAD_DIM)
            o_ref[0, rows, hs] = (o[vh] * lax.rsqrt(ms[vh] + EPS) * nw * _silu(zf[:, hs])).astype(o_ref.dtype)
    state_ref[...] = state


def gdn_core(qkv_act, proj, beta, gc, gl, gct, norm_w, n_k_heads, n_chunks=4, nkh=2):
    b, s, _ = qkv_act.shape
    nkh = min(nkh, n_k_heads)
    assert n_k_heads % nkh == 0
    g = n_k_heads // nkh
    kw, vw = nkh * HEAD_DIM, 2 * nkh * HEAD_DIM
    n_chunks = min(n_chunks, s // GDN_CHUNK)
    t = n_chunks * GDN_CHUNK
    gate = pl.BlockSpec((1, t, LANES), lambda bi, gi, si: (bi, si, 0))
    return pl.pallas_call(
        functools.partial(_gdn_kernel, n_chunks=n_chunks, nkh=nkh),
        grid=(b, g, s // t),
        in_specs=[pl.BlockSpec((1, t, kw), lambda bi, gi, si: (bi, si, gi)),
                  pl.BlockSpec((1, t, kw), lambda bi, gi, si: (bi, si, g + gi)),
                  pl.BlockSpec((1, t, vw), lambda bi, gi, si: (bi, si, g + gi)),
                  pl.BlockSpec((1, t, vw), lambda bi, gi, si: (bi, si, 2 * g + gi)),
                  gate, gate, gate,
                  pl.BlockSpec((1, nkh, n_chunks, 2, GDN_CHUNK), lambda bi, gi, si: (bi, gi, si, 0, 0)),
                  pl.BlockSpec((1, HEAD_DIM), lambda bi, gi, si: (0, 0))],
        out_specs=pl.BlockSpec((1, t, vw), lambda bi, gi, si: (bi, si, gi)),
        out_shape=jax.ShapeDtypeStruct((b, s, 2 * n_k_heads * HEAD_DIM), BF16),
        scratch_shapes=[pltpu.VMEM((2 * nkh, HEAD_DIM, HEAD_DIM), F32)],
        compiler_params=_params("parallel", "parallel", "arbitrary"),
        name="gdn_core",
    )(qkv_act, qkv_act, qkv_act, proj, beta, gc, gl, gct, norm_w)


def _router_kernel(logit_ref, idx_ref, w_ref, *, n_experts):
    x = logit_ref[...]
    lane = lax.broadcasted_iota(jnp.int32, x.shape, 1)
    x = jnp.where(lane < n_experts, x, -jnp.inf)
    m1 = jnp.max(x, axis=1, keepdims=True)
    i1 = jnp.min(jnp.where(x == m1, lane, LANES), axis=1, keepdims=True)
    x2 = jnp.where(lane == i1, -jnp.inf, x)
    m2 = jnp.max(x2, axis=1, keepdims=True)
    i2 = jnp.min(jnp.where(x2 == m2, lane, LANES), axis=1, keepdims=True)
    e2 = jnp.exp(m2 - m1)
    w1 = 1.0 / (1.0 + e2)
    w2 = e2 / (1.0 + e2)
    idx_ref[...] = jnp.where(lane == 0, i1, jnp.where(lane == 1, i2, 0))
    w_ref[...] = jnp.where(lane == 0, w1, jnp.where(lane == 1, w2, 0.0))


def router_top2(logits, n_experts, tm=1024):
    m, n = logits.shape
    spec = pl.BlockSpec((tm, n), lambda i: (i, 0))
    return pl.pallas_call(
        functools.partial(_router_kernel, n_experts=n_experts),
        grid=(m // tm,),
        in_specs=[spec],
        out_specs=[spec, spec],
        out_shape=[jax.ShapeDtypeStruct((m, n), jnp.int32), jax.ShapeDtypeStruct((m, n), F32)],
        compiler_params=_params("parallel"),
        name="router",
    )(logits)


MOE_ROW_TILE = 256


def _route(idx, n_experts, tm):
    m = idx.shape[0]
    a = 2 * m
    e = idx.reshape(a)
    onehot = (e[:, None] == jnp.arange(n_experts, dtype=jnp.int32)[None, :]).astype(jnp.int32)
    csum = jnp.cumsum(onehot, axis=0)
    counts = csum[-1]
    padded = (counts + tm - 1) // tm * tm
    ends = jnp.cumsum(padded)
    pos = jnp.sum(onehot * (csum - 1 + (ends - padded)[None, :]), axis=1)
    n_tiles = a // tm + n_experts
    tile_start = jnp.arange(n_tiles, dtype=jnp.int32) * tm
    tile_expert = jnp.minimum(jnp.sum((tile_start[:, None] >= ends[None, :]).astype(jnp.int32), axis=1),
                              n_experts - 1)
    n_used = (ends[-1] // tm).astype(jnp.int32).reshape(1)
    row_token = jnp.zeros((n_tiles * tm,), jnp.int32).at[pos].set(jnp.arange(a, dtype=jnp.int32) // 2)
    return pos.astype(jnp.int32), tile_expert, n_used, row_token


DMA_ISSUE_UNROLL = 8


def _row_copy(src_hbm, src_row, dst_vmem, dst_row, sem):
    return pltpu.make_async_copy(src_hbm.at[pl.ds(src_row, 1)], dst_vmem.at[pl.ds(dst_row, 1)], sem)


def _rows_wait(src_hbm, dst_vmem, sem):
    pltpu.make_async_copy(src_hbm.at[pl.ds(0, dst_vmem.shape[0])], dst_vmem, sem).wait()


def _moe_gather_kernel(row_token_ref, n_used_ref, h_hbm, g_ref, o_ref, buf, sem, *, tm):
    i = pl.program_id(0)
    n_used = n_used_ref[0]

    def issue(tile):
        slot = lax.rem(tile, 2)

        def start(r, c):
            _row_copy(h_hbm, row_token_ref[tile * tm + r], buf.at[slot], r, sem.at[slot]).start()
            return c

        lax.fori_loop(0, tm, start, 0, unroll=DMA_ISSUE_UNROLL)

    @pl.when(i == 0)
    def _():
        issue(i)

    @pl.when(i + 1 < n_used)
    def _():
        issue(i + 1)

    @pl.when(i < n_used)
    def _():
        slot = lax.rem(i, 2)
        _rows_wait(h_hbm, buf.at[slot], sem.at[slot])
        x = buf[slot]
        ms = jnp.mean(x * x, axis=-1, keepdims=True)
        o_ref[...] = (x * lax.rsqrt(ms + EPS) * g_ref[...]).astype(o_ref.dtype)

    @pl.when(i >= n_used)
    def _():
        o_ref[...] = jnp.zeros_like(o_ref)


def moe_gather(h, g, row_token, n_used, tm):
    m, d = h.shape
    p = row_token.shape[0]
    return pl.pallas_call(
        functools.partial(_moe_gather_kernel, tm=tm),
        grid_spec=pltpu.PrefetchScalarGridSpec(
            num_scalar_prefetch=2,
            grid=(p // tm,),
            in_specs=[pl.BlockSpec(memory_space=pl.ANY),
                      pl.BlockSpec((1, d), lambda i, rt, nu: (0, 0))],
            out_specs=pl.BlockSpec((tm, d), lambda i, rt, nu: (i, 0)),
            scratch_shapes=[pltpu.VMEM((2, tm, d), F32), pltpu.SemaphoreType.DMA((2,))]),
        out_shape=jax.ShapeDtypeStruct((p, d), BF16),
        compiler_params=_params("arbitrary"),
        name="moe_gather",
    )(row_token, n_used, h, g.reshape(1, d))


def _moe_up_kernel(te_ref, n_used_ref, x_ref, wg_ref, wu_ref, o_ref):
    i = pl.program_id(1)

    @pl.when(i < n_used_ref[0])
    def _():
        x = x_ref[...]
        gate = _dot(x, wg_ref[...])
        up = _dot(x, wu_ref[...])
        o_ref[...] = (_silu(gate) * up).astype(o_ref.dtype)

    @pl.when(i >= n_used_ref[0])
    def _():
        o_ref[...] = jnp.zeros_like(o_ref)


def moe_up(xs, w_gate_up, layer, tile_expert, n_used, tm, tn=1408):
    p, d = xs.shape
    f = w_gate_up.shape[3] // 2
    tn = _tile(f, tn)
    nj = f // tn

    def row_tile(j, i, te, nu):
        return (jnp.minimum(i, nu[0] - 1), 0)

    return pl.pallas_call(
        _moe_up_kernel,
        grid_spec=pltpu.PrefetchScalarGridSpec(
            num_scalar_prefetch=2,
            grid=(nj, p // tm),
            in_specs=[pl.BlockSpec((tm, d), row_tile),
                      pl.BlockSpec((None, None, d, tn), lambda j, i, te, nu: (layer, te[i], 0, j)),
                      pl.BlockSpec((None, None, d, tn), lambda j, i, te, nu: (layer, te[i], 0, j + nj))],
            out_specs=pl.BlockSpec((tm, tn), lambda j, i, te, nu: (i, j))),
        out_shape=jax.ShapeDtypeStruct((p, f), BF16),
        compiler_params=_params("parallel", "arbitrary"),
        name="moe_up",
    )(tile_expert, n_used, xs, w_gate_up, w_gate_up)


def _moe_down_kernel(te_ref, n_used_ref, x_ref, w_ref, o_ref):
    i = pl.program_id(0)

    @pl.when(i < n_used_ref[0])
    def _():
        o_ref[...] = _dot(x_ref[...], w_ref[...])

    @pl.when(i >= n_used_ref[0])
    def _():
        o_ref[...] = jnp.zeros_like(o_ref)


def moe_down(hs, w_down, layer, tile_expert, n_used, tm):
    p, f = hs.shape
    d = w_down.shape[3]
    return pl.pallas_call(
        _moe_down_kernel,
        grid_spec=pltpu.PrefetchScalarGridSpec(
            num_scalar_prefetch=2,
            grid=(p // tm,),
            in_specs=[pl.BlockSpec((tm, f), lambda i, te, nu: (jnp.minimum(i, nu[0] - 1), 0)),
                      pl.BlockSpec((None, None, f, d), lambda i, te, nu: (layer, te[i], 0, 0))],
            out_specs=pl.BlockSpec((tm, d), lambda i, te, nu: (i, 0))),
        out_shape=jax.ShapeDtypeStruct((p, d), F32),
        compiler_params=_params("arbitrary"),
        name="moe_down",
    )(tile_expert, n_used, hs, w_down)


def _moe_combine_kernel(pos_ref, h_ref, w_ref, y_hbm, o_ref, buf, sem, *, tm):
    i = pl.program_id(0)

    def issue(tile):
        slot = lax.rem(tile, 2)

        def start(r, c):
            for k in range(2):
                _row_copy(y_hbm, pos_ref[2 * (tile * tm + r) + k], buf.at[slot, k], r, sem.at[slot]).start()
            return c

        lax.fori_loop(0, tm, start, 0, unroll=DMA_ISSUE_UNROLL)

    @pl.when(i == 0)
    def _():
        issue(i)

    @pl.when(i + 1 < pl.num_programs(0))
    def _():
        issue(i + 1)

    slot = lax.rem(i, 2)
    for k in range(2):
        _rows_wait(y_hbm, buf.at[slot, k], sem.at[slot])
    w = w_ref[...]
    o_ref[...] = h_ref[...] + w[:, 0:1] * buf[slot, 0] + w[:, 1:2] * buf[slot, 1]


def moe_combine(h, top_w, ys, pos, tm=256):
    m, d = h.shape
    return pl.pallas_call(
        functools.partial(_moe_combine_kernel, tm=tm),
        grid_spec=pltpu.PrefetchScalarGridSpec(
            num_scalar_prefetch=1,
            grid=(m // tm,),
            in_specs=[pl.BlockSpec((tm, d), lambda i, pos: (i, 0)),
                      pl.BlockSpec((tm, LANES), lambda i, pos: (i, 0)),
                      pl.BlockSpec(memory_space=pl.ANY)],
            out_specs=pl.BlockSpec((tm, d), lambda i, pos: (i, 0)),
            scratch_shapes=[pltpu.VMEM((2, 2, tm, d), F32), pltpu.SemaphoreType.DMA((2,))]),
        out_shape=jax.ShapeDtypeStruct((m, d), F32),
        compiler_params=_params("arbitrary"),
        name="moe_combine",
    )(pos, h, top_w, ys)


def _sb_layer(h, g, w_in, w_out, layer):
    b, s, d = h.shape
    n_heads = w_out.shape[1] // HEAD_DIM
    hn = rmsnorm(h.reshape(b * s, d), g, BF16)
    qkv = matmul(hn, w_in, layer, w_in.shape[2], BF16)
    o = sb_attention(qkv.reshape(b, s, -1), n_heads)
    return matmul(o.reshape(b * s, -1), w_out, layer, d, F32, res=h.reshape(b * s, d)).reshape(b, s, d)


def _fox_layer(h, g, w_in, b_f, w_out, layer):
    b, s, d = h.shape
    n_heads = w_out.shape[1] // HEAD_DIM
    main = 4 * n_heads * HEAD_DIM
    h2 = h.reshape(b * s, d)
    hn = rmsnorm(h2, g, BF16)
    proj = matmul(hn, w_in, layer, main, BF16).reshape(b, s, main)
    f_raw = small_proj(h2, g, _pad_cols(w_in[layer, :, main:], LANES)).reshape(b, s, LANES)
    bias = jnp.pad(b_f, (0, LANES - n_heads)).reshape(1, LANES)
    cum_col, cum_t = fox_gate(f_raw, bias)
    o = fox_attention(proj, cum_col, cum_t[:, :n_heads, :], n_heads)
    return matmul(o.reshape(b * s, -1), w_out, layer, d, F32, res=h2).reshape(b, s, d)


def _gdn_layer(h, g, w_in, conv_w, a_log, dt_bias, norm_w, w_out, layer):
    b, s, d = h.shape
    v_dim = w_out.shape[1]
    n_v_heads = v_dim // HEAD_DIM
    n_k_heads = n_v_heads // 2
    k_dim = n_k_heads * HEAD_DIM
    qkv_dim = 2 * k_dim + v_dim
    main = qkv_dim + v_dim
    h2 = h.reshape(b * s, d)
    hn = rmsnorm(h2, g, BF16)
    proj = matmul(hn, w_in, layer, main, BF16).reshape(b, s, main)
    w_gates = jnp.concatenate([_pad_cols(w_in[layer, :, main:main + n_v_heads], LANES),
                               _pad_cols(w_in[layer, :, main + n_v_heads:], LANES)], axis=1)
    raw = small_proj(h2, g, w_gates).reshape(b, s, 2 * LANES)
    pad = (0, LANES - n_v_heads)
    beta, gc, gl, gct = gdn_gates(raw[..., :LANES], raw[..., LANES:],
                                  jnp.pad(a_log, pad).reshape(1, LANES), jnp.pad(dt_bias, pad).reshape(1, LANES))
    gct = gct[:, :n_v_heads, :].reshape(b, n_k_heads, 2, s // GDN_CHUNK, GDN_CHUNK).transpose(0, 1, 3, 2, 4)
    qkv_act = gdn_prep(proj, conv_w.T, qkv_dim, k_dim)
    o = gdn_core(qkv_act, proj, beta, gc, gl, gct, norm_w.reshape(1, HEAD_DIM), n_k_heads)
    return matmul(o.reshape(b * s, v_dim), w_out, layer, d, F32, res=h2).reshape(b, s, d)


def _dense_ffn(h, g, w_gate_up, w_down, layer):
    b, s, d = h.shape
    h2 = h.reshape(b * s, d)
    hn = rmsnorm(h2, g, BF16)
    mid = swiglu_up(hn, w_gate_up, layer)
    return matmul(mid, w_down, layer, d, F32, res=h2).reshape(b, s, d)


def _moe_ffn(h, g, w_router, w_gate_up, w_down, layer):
    b, s, d = h.shape
    n_experts = w_router.shape[1]
    h2 = h.reshape(b * s, d)
    logits = small_proj(h2, g, _pad_cols(w_router, LANES))
    top_idx, top_w = router_top2(logits, n_experts)
    tm = MOE_ROW_TILE
    pos, tile_expert, n_used, row_token = _route(top_idx[:, :2], n_experts, tm)
    xs = moe_gather(h2, g, row_token, n_used, tm)
    mid = moe_up(xs, w_gate_up, layer, tile_expert, n_used, tm)
    ys = moe_down(mid, w_down, layer, tile_expert, n_used, tm)
    return moe_combine(h2, top_w, ys, pos).reshape(b, s, d)


def kernel(x, norm_mix, norm_ffn, sb_w_in, sb_w_out, gdn_w_in, gdn_conv_w, gdn_a_log, gdn_dt_bias, gdn_norm_w,
           gdn_w_out, fox_w_in, fox_b_f, fox_w_out, ffn_w_gate_up, ffn_w_down, moe_w_router, moe_w_gate_up,
           moe_w_down, final_norm):
    depth = norm_mix.shape[0]
    moe_gate_up_b = moe_w_gate_up.astype(BF16)
    moe_down_b = moe_w_down.astype(BF16)
    h = x
    for i in range(depth):
        kind, j = i % 3, i // 3
        if kind == 0:
            h = _sb_layer(h, norm_mix[i], sb_w_in, sb_w_out, j)
        elif kind == 1:
            h = _gdn_layer(h, norm_mix[i], gdn_w_in, gdn_conv_w[j], gdn_a_log[j], gdn_dt_bias[j],
                           gdn_norm_w[j], gdn_w_out, j)
        else:
            h = _fox_layer(h, norm_mix[i], fox_w_in, fox_b_f[j], fox_w_out, j)
        f = i // 2
        if i % 2 == 0:
            h = _dense_ffn(h, norm_ffn[i], ffn_w_gate_up, ffn_w_down, f)
        else:
            h = _moe_ffn(h, norm_ffn[i], moe_w_router[f], moe_gate_up_b, moe_down_b, f)
    b, s, d = h.shape
    return rmsnorm(h.reshape(b * s, d), final_norm, x.dtype).reshape(b, s, d)
```

```python
import functools

import jax
import jax.numpy as jnp
from jax import lax
from jax.experimental import pallas as pl
from jax.experimental.pallas import tpu as pltpu

HEAD_DIM = 128
LANES = 128
GDN_CONV = 4
GDN_CHUNK = 128
EPS = 1e-6
LOG2E = 1.4426950408889634
EXP_UNDERFLOW = -104.0
VMEM_LIMIT_BYTES = 56 * 1024 * 1024

F32 = jnp.float32
BF16 = jnp.bfloat16


def _params(*semantics):
    return pltpu.CompilerParams(dimension_semantics=semantics, vmem_limit_bytes=VMEM_LIMIT_BYTES)


def _tile(dim, preferred):
    assert dim % LANES == 0, dim
    t = min(preferred, dim) // LANES * LANES
    while dim % t:
        t -= LANES
    return t


def _dot(a, b):
    return jnp.dot(a, b, preferred_element_type=F32)


def _split3(x):
    hi = x.astype(BF16)
    r = x - hi.astype(F32)
    mid = r.astype(BF16)
    lo = (r - mid.astype(F32)).astype(BF16)
    return hi, mid, lo


def _split2(x):
    hi = x.astype(BF16)
    lo = (x - hi.astype(F32)).astype(BF16)
    return hi, lo


def _softplus(z):
    return jnp.maximum(z, 0.0) + jnp.log(1.0 + jnp.exp(-jnp.abs(z)))


def _sigmoid(z):
    return 1.0 / (1.0 + jnp.exp(-z))


def _silu(z):
    return z * _sigmoid(z)


def _block_id(idx, size):
    shift = size.bit_length() - 1
    assert 1 << shift == size
    return jnp.right_shift(idx, shift)


def _rmsnorm_kernel(x_ref, g_ref, o_ref):
    x = x_ref[...]
    ms = jnp.mean(x * x, axis=-1, keepdims=True)
    o_ref[...] = (x * lax.rsqrt(ms + EPS) * g_ref[...]).astype(o_ref.dtype)


def rmsnorm(x, g, out_dtype, tm=512):
    m, d = x.shape
    return pl.pallas_call(
        _rmsnorm_kernel,
        grid=(m // tm,),
        in_specs=[pl.BlockSpec((tm, d), lambda i: (i, 0)),
                  pl.BlockSpec((1, d), lambda i: (0, 0))],
        out_specs=pl.BlockSpec((tm, d), lambda i: (i, 0)),
        out_shape=jax.ShapeDtypeStruct((m, d), out_dtype),
        compiler_params=_params("parallel"),
        name="rmsnorm",
    )(x, g.reshape(1, d))


WS_VMEM_BUDGET = 44 * 1024 * 1024


def _ws_tiles(m, kdim, n, n_weight_blocks, out_bytes, has_res):
    for tm_pref, tn_pref in ((1024, 1024), (1024, 512), (512, 512), (512, 256), (256, 256), (256, 128)):
        tm, tn = _tile(m, tm_pref), _tile(n, tn_pref)
        weights = n_weight_blocks * kdim * tn * (2 * 4 + 2)
        acts = 2 * tm * kdim * 2
        outs = 2 * tm * tn * (out_bytes + (4 if has_res else 0))
        results = n_weight_blocks * tm * tn * 4
        if weights + acts + outs + results <= WS_VMEM_BUDGET:
            return tm, tn
    raise ValueError(f"no weight-stationary tiling fits VMEM for {(m, kdim, n)}")


def _matmul_kernel(*refs, has_res):
    if has_res:
        a_ref, w_ref, res_ref, o_ref, wb_ref = refs
    else:
        a_ref, w_ref, o_ref, wb_ref = refs

    @pl.when(pl.program_id(1) == 0)
    def _():
        wb_ref[...] = w_ref[...].astype(BF16)

    r = _dot(a_ref[...], wb_ref[...])
    if has_res:
        r = r + res_ref[...]
    o_ref[...] = r.astype(o_ref.dtype)


def matmul(a, w, layer, n_cols, out_dtype, res=None):
    m, kdim = a.shape
    tm, tn = _ws_tiles(m, kdim, n_cols, 1, jnp.dtype(out_dtype).itemsize, res is not None)
    in_specs = [pl.BlockSpec((tm, kdim), lambda j, i: (i, 0)),
                pl.BlockSpec((kdim, tn), lambda j, i: (layer, j))]
    args = [a, w.reshape(-1, w.shape[2])]
    if res is not None:
        in_specs.append(pl.BlockSpec((tm, tn), lambda j, i: (i, j)))
        args.append(res)
    return pl.pallas_call(
        functools.partial(_matmul_kernel, has_res=res is not None),
        grid=(n_cols // tn, m // tm),
        in_specs=in_specs,
        out_specs=pl.BlockSpec((tm, tn), lambda j, i: (i, j)),
        out_shape=jax.ShapeDtypeStruct((m, n_cols), out_dtype),
        scratch_shapes=[pltpu.VMEM((kdim, tn), BF16)],
        compiler_params=_params("parallel", "arbitrary"),
        name="matmul",
    )(*args)


def _swiglu_up_kernel(a_ref, wg_ref, wu_ref, o_ref, wgb_ref, wub_ref):
    @pl.when(pl.program_id(1) == 0)
    def _():
        wgb_ref[...] = wg_ref[...].astype(BF16)
        wub_ref[...] = wu_ref[...].astype(BF16)

    a = a_ref[...]
    o_ref[...] = (_silu(_dot(a, wgb_ref[...])) * _dot(a, wub_ref[...])).astype(o_ref.dtype)


def swiglu_up(a, w_gate_up, layer):
    m, kdim = a.shape
    f = w_gate_up.shape[2] // 2
    tm, tn = _ws_tiles(m, kdim, f, 2, 2, False)
    nj = f // tn
    return pl.pallas_call(
        _swiglu_up_kernel,
        grid=(nj, m // tm),
        in_specs=[pl.BlockSpec((tm, kdim), lambda j, i: (i, 0)),
                  pl.BlockSpec((None, kdim, tn), lambda j, i: (layer, 0, j)),
                  pl.BlockSpec((None, kdim, tn), lambda j, i: (layer, 0, j + nj))],
        out_specs=pl.BlockSpec((tm, tn), lambda j, i: (i, j)),
        out_shape=jax.ShapeDtypeStruct((m, f), BF16),
        scratch_shapes=[pltpu.VMEM((kdim, tn), BF16), pltpu.VMEM((kdim, tn), BF16)],
        compiler_params=_params("parallel", "arbitrary"),
        name="swiglu_up",
    )(a, w_gate_up, w_gate_up)


def _small_proj_kernel(h_ref, g_ref, w_ref, o_ref):
    x = h_ref[...]
    ms = jnp.mean(x * x, axis=-1, keepdims=True)
    xn = x * lax.rsqrt(ms + EPS) * g_ref[...]
    xh, xl = _split2(xn)
    wh, wl = _split2(w_ref[...])
    o_ref[...] = _dot(xh, wh) + _dot(xl, wh) + _dot(xh, wl)


def small_proj(h, g, w, tm=512):
    m, d = h.shape
    n = w.shape[1]
    return pl.pallas_call(
        _small_proj_kernel,
        grid=(m // tm,),
        in_specs=[pl.BlockSpec((tm, d), lambda i: (i, 0)),
                  pl.BlockSpec((1, d), lambda i: (0, 0)),
                  pl.BlockSpec((d, n), lambda i: (0, 0))],
        out_specs=pl.BlockSpec((tm, n), lambda i: (i, 0)),
        out_shape=jax.ShapeDtypeStruct((m, n), F32),
        compiler_params=_params("parallel"),
        name="small_proj",
    )(h, g.reshape(1, d), w)


def _pad_cols(w, width):
    return jnp.pad(w, ((0, 0), (0, width - w.shape[1])))


def _split_heads(x, n):
    return jnp.stack([x[:, h * HEAD_DIM:(h + 1) * HEAD_DIM] for h in range(n)], axis=0)


def _sb_attn_kernel(q_ref, k_ref, v_ref, o_ref, *, t, hb, scale):
    i = pl.program_id(2)
    q = _split_heads(q_ref[0], hb)
    row = lax.broadcasted_iota(jnp.int32, (t, t), 0)
    col = lax.broadcasted_iota(jnp.int32, (t, t), 1)
    strict = col < row
    suffix = jnp.where(row > col, 1.0, 0.0).astype(BF16)

    def block(kb, carry, acc, masked):
        start = pl.multiple_of(kb * t, t)
        k = _split_heads(k_ref[0, pl.ds(start, t), :], hb)
        v = _split_heads(v_ref[0, pl.ds(start, t), :], hb)
        z = _bmm_nt(q, k) * scale
        sp = _softplus(z)
        log_keep = -sp
        if masked:
            log_keep = jnp.where(strict, log_keep, 0.0)
        hi, lo = _split2(log_keep.reshape(hb * t, t))
        later = (_dot(hi, suffix) + _dot(lo, suffix)).reshape(hb, t, t) + carry
        w = jnp.exp(z - sp + later)
        if masked:
            w = jnp.where(strict, w, 0.0)
        acc = acc + _bmm(w.astype(BF16), v)
        carry = carry + jnp.sum(log_keep, axis=2, keepdims=True)
        return carry, acc

    carry, acc = block(i, jnp.zeros((hb, t, 1), F32), jnp.zeros((hb, t, HEAD_DIM), F32), True)

    def live(c):
        return (c[0] < i) & (c[3] > EXP_UNDERFLOW)

    def body(c):
        carry, acc = block(i - 1 - c[0], c[1], c[2], False)
        return c[0] + 1, carry, acc, jnp.max(carry)

    _, carry, acc, _ = lax.while_loop(live, body, (jnp.int32(0), carry, acc, jnp.max(carry)))
    for h in range(hb):
        o_ref[0, :, h * HEAD_DIM:(h + 1) * HEAD_DIM] = acc[h].astype(o_ref.dtype)


def sb_attention(qkv, n_heads, t=256, hb=4):
    b, s, _ = qkv.shape
    hb = min(hb, n_heads)
    assert n_heads % hb == 0
    g = n_heads // hb
    w = hb * HEAD_DIM
    return pl.pallas_call(
        functools.partial(_sb_attn_kernel, t=t, hb=hb, scale=HEAD_DIM ** -0.5),
        grid=(b, g, s // t),
        in_specs=[pl.BlockSpec((1, t, w), lambda bi, gi, i: (bi, i, gi)),
                  pl.BlockSpec((1, s, w), lambda bi, gi, i: (bi, 0, g + gi)),
                  pl.BlockSpec((1, s, w), lambda bi, gi, i: (bi, 0, 2 * g + gi))],
        out_specs=pl.BlockSpec((1, t, w), lambda bi, gi, i: (bi, i, gi)),
        out_shape=jax.ShapeDtypeStruct((b, s, n_heads * HEAD_DIM), BF16),
        compiler_params=_params("parallel", "parallel", "arbitrary"),
        name="sb_attention",
    )(qkv, qkv, qkv)


def _fox_gate_kernel(f_ref, bias_ref, col_ref, row_ref, carry_ref, *, t):
    i = pl.program_id(1)

    @pl.when(i == 0)
    def _():
        carry_ref[...] = jnp.zeros_like(carry_ref)

    x = f_ref[0] + bias_ref[...]
    log_f = -_softplus(-x)
    row = lax.broadcasted_iota(jnp.int32, (t, t), 0)
    col = lax.broadcasted_iota(jnp.int32, (t, t), 1)
    lower = jnp.where(col <= row, 1.0, 0.0).astype(BF16)
    hi, mid, lo = _split3(log_f)
    cum = _dot(lower, hi) + _dot(lower, mid) + _dot(lower, lo) + carry_ref[...]
    col_ref[0] = cum
    row_ref[0] = cum.T
    carry_ref[...] = cum[t - 1:t, :]


def fox_gate(f_raw, bias, t=512):
    b, s, n = f_raw.shape
    return pl.pallas_call(
        functools.partial(_fox_gate_kernel, t=t),
        grid=(b, s // t),
        in_specs=[pl.BlockSpec((1, t, n), lambda bi, i: (bi, i, 0)),
                  pl.BlockSpec((1, n), lambda bi, i: (0, 0))],
        out_specs=[pl.BlockSpec((1, t, n), lambda bi, i: (bi, i, 0)),
                   pl.BlockSpec((1, n, t), lambda bi, i: (bi, 0, i))],
        out_shape=[jax.ShapeDtypeStruct((b, s, n), F32), jax.ShapeDtypeStruct((b, n, s), F32)],
        scratch_shapes=[pltpu.VMEM((1, n), F32)],
        compiler_params=_params("parallel", "arbitrary"),
        name="fox_gate",
    )(f_raw, bias)


def _fox_attn_kernel(q_ref, k_ref, v_ref, gate_ref, cq_ref, ck_ref, o_ref, *, t, hb, scale):
    gi = pl.program_id(1)
    i = pl.program_id(2)
    q = _split_heads(q_ref[0], hb)
    lane = lax.broadcasted_iota(jnp.int32, (t, LANES), 1)
    cq = cq_ref[0]
    cum_q = jnp.stack([jnp.sum(jnp.where(lane == gi * hb + h, cq, 0.0), axis=1, keepdims=True)
                       for h in range(hb)], axis=0) * LOG2E
    row = lax.broadcasted_iota(jnp.int32, (t, t), 0)
    col = lax.broadcasted_iota(jnp.int32, (t, t), 1)
    causal = col <= row

    def block(kb, m, l, acc, masked):
        start = pl.multiple_of(kb * t, t)
        k = _split_heads(k_ref[0, pl.ds(start, t), :], hb)
        v = _split_heads(v_ref[0, pl.ds(start, t), :], hb)
        cum_k = ck_ref[0, :, pl.ds(kb, 1), :] * LOG2E
        logits = _bmm_nt(q, k) * (scale * LOG2E) + cum_q - cum_k
        if masked:
            logits = jnp.where(causal, logits, -jnp.inf)
        m_new = jnp.maximum(m, jnp.max(logits, axis=2, keepdims=True))
        alpha = jnp.exp2(m - m_new)
        p = jnp.exp2(logits - m_new)
        l = l * alpha + jnp.sum(p, axis=2, keepdims=True)
        acc = acc * alpha + _bmm(p.astype(BF16), v)
        return m_new, l, acc

    init = (jnp.full((hb, t, 1), -jnp.inf, F32), jnp.zeros((hb, t, 1), F32), jnp.zeros((hb, t, HEAD_DIM), F32))
    m, l, acc = block(i, *init, True)

    def body(n, c):
        return block(i - 1 - n, *c, False)

    m, l, acc = lax.fori_loop(0, i, body, (m, l, acc))
    out = acc / l
    for h in range(hb):
        hs = slice(h * HEAD_DIM, (h + 1) * HEAD_DIM)
        o_ref[0, :, hs] = (out[h] * _sigmoid(gate_ref[0, :, hs].astype(F32))).astype(o_ref.dtype)


def fox_attention(proj, cum_col, cum_row, n_heads, t=512, hb=2):
    b, s, _ = proj.shape
    hb = min(hb, n_heads)
    assert n_heads % hb == 0
    g = n_heads // hb
    w = hb * HEAD_DIM
    cum_row = cum_row.reshape(b, n_heads, s // t, t)
    return pl.pallas_call(
        functools.partial(_fox_attn_kernel, t=t, hb=hb, scale=HEAD_DIM ** -0.5),
        grid=(b, g, s // t),
        in_specs=[pl.BlockSpec((1, t, w), lambda bi, gi, i: (bi, i, gi)),
                  pl.BlockSpec((1, s, w), lambda bi, gi, i: (bi, 0, g + gi)),
                  pl.BlockSpec((1, s, w), lambda bi, gi, i: (bi, 0, 2 * g + gi)),
                  pl.BlockSpec((1, t, w), lambda bi, gi, i: (bi, i, 3 * g + gi)),
                  pl.BlockSpec((1, t, LANES), lambda bi, gi, i: (bi, i, 0)),
                  pl.BlockSpec((1, hb, s // t, t), lambda bi, gi, i: (bi, gi, 0, 0))],
        out_specs=pl.BlockSpec((1, t, w), lambda bi, gi, i: (bi, i, gi)),
        out_shape=jax.ShapeDtypeStruct((b, s, n_heads * HEAD_DIM), BF16),
        compiler_params=_params("parallel", "parallel", "arbitrary"),
        name="fox_attention",
    )(proj, proj, proj, proj, cum_col, cum_row)


def _gdn_gate_kernel(b_ref, a_ref, alog_ref, dt_ref, beta_ref, gc_ref, gl_ref, gct_ref, *, t, chunk):
    beta_ref[0] = _sigmoid(b_ref[0])
    g = -jnp.exp(alog_ref[...]) * _softplus(a_ref[0] + dt_ref[...])
    row = lax.broadcasted_iota(jnp.int32, (t, t), 0)
    col = lax.broadcasted_iota(jnp.int32, (t, t), 1)
    same = _block_id(row, chunk) == _block_id(col, chunk)
    lower = jnp.where(same & (col <= row), 1.0, 0.0).astype(BF16)
    whole = jnp.where(same, 1.0, 0.0).astype(BF16)
    hi, mid, lo = _split3(g)
    gc = _dot(lower, hi) + _dot(lower, mid) + _dot(lower, lo)
    gc_ref[0] = gc
    gl_ref[0] = _dot(whole, hi) + _dot(whole, mid) + _dot(whole, lo)
    gct_ref[0] = gc.T


def gdn_gates(b_raw, a_raw, a_log, dt_bias, t=512):
    b, s, n = b_raw.shape
    spec = pl.BlockSpec((1, t, n), lambda bi, i: (bi, i, 0))
    vec = pl.BlockSpec((1, n), lambda bi, i: (0, 0))
    return pl.pallas_call(
        functools.partial(_gdn_gate_kernel, t=t, chunk=GDN_CHUNK),
        grid=(b, s // t),
        in_specs=[spec, spec, vec, vec],
        out_specs=[spec, spec, spec, pl.BlockSpec((1, n, t), lambda bi, i: (bi, 0, i))],
        out_shape=[jax.ShapeDtypeStruct((b, s, n), F32)] * 3 + [jax.ShapeDtypeStruct((b, n, s), F32)],
        compiler_params=_params("parallel", "parallel"),
        name="gdn_gates",
    )(b_raw, a_raw, a_log, dt_bias)


def _gdn_prep_kernel(x_ref, prev_ref, w_ref, o_ref, *, ts, n_q_blocks, n_qk_blocks, heads_per_block):
    si = pl.program_id(1)
    ci = pl.program_id(2)
    cur = x_ref[0].astype(F32)
    prev = jnp.where(si == 0, 0.0, prev_ref[0].astype(F32))
    ext = jnp.concatenate([prev, cur], axis=0)
    w = w_ref[...]
    y = jnp.zeros_like(cur)
    for tap in range(GDN_CONV):
        off = 8 - (GDN_CONV - 1) + tap
        y = y + w[tap:tap + 1, :] * ext[off:off + ts, :]
    y = _silu(y)
    q_scale = jnp.where(ci < n_q_blocks, HEAD_DIM ** -0.5, 1.0)
    is_qk = ci < n_qk_blocks
    for hh in range(heads_per_block):
        sl = slice(hh * HEAD_DIM, (hh + 1) * HEAD_DIM)
        yh = y[:, sl]
        ss = jnp.sum(yh * yh, axis=1, keepdims=True)
        factor = jnp.where(is_qk, lax.rsqrt(ss + EPS) * q_scale, 1.0)
        o_ref[0, :, sl] = (yh * factor).astype(o_ref.dtype)


def gdn_prep(proj, conv_w_t, qkv_dim, k_dim, ts=512, tc=512):
    b, s, _ = proj.shape
    ts, tc = min(ts, s), min(tc, k_dim)
    rows8 = ts // 8
    return pl.pallas_call(
        functools.partial(_gdn_prep_kernel, ts=ts, n_q_blocks=k_dim // tc, n_qk_blocks=2 * k_dim // tc,
                          heads_per_block=tc // HEAD_DIM),
        grid=(b, s // ts, qkv_dim // tc),
        in_specs=[pl.BlockSpec((1, ts, tc), lambda bi, si, ci: (bi, si, ci)),
                  pl.BlockSpec((1, 8, tc), lambda bi, si, ci: (bi, jnp.maximum(si * rows8 - 1, 0), ci)),
                  pl.BlockSpec((GDN_CONV, tc), lambda bi, si, ci: (0, ci))],
        out_specs=pl.BlockSpec((1, ts, tc), lambda bi, si, ci: (bi, si, ci)),
        out_shape=jax.ShapeDtypeStruct((b, s, qkv_dim), BF16),
        compiler_params=_params("parallel", "parallel", "parallel"),
        name="gdn_prep",
    )(proj, proj, conv_w_t)


def _bmm(a, b):
    return lax.dot_general(a, b, (((2,), (1,)), ((0,), (0,))), preferred_element_type=F32)


def _bmm_nt(a, b):
    return lax.dot_general(a, b, (((2,), (2,)), ((0,), (0,))), preferred_element_type=F32)


def _bmm_tn(a, b):
    return lax.dot_general(a, b, (((1,), (1,)), ((0,), (0,))), preferred_element_type=F32)


def _inv_unit_lower(m_strict, row, col):
    c = m_strict.shape[-1]
    eye = jnp.where(row == col, 1.0, 0.0)
    n1 = jnp.where(_block_id(row, 8) == _block_id(col, 8), -m_strict, 0.0)
    n1b = n1.astype(BF16)
    n2 = _bmm(n1b, n1b)
    n2b = n2.astype(BF16)
    n4 = _bmm(n2b, n2b)
    x = _bmm((eye + n1).astype(BF16), (eye + n2).astype(BF16))
    x = _bmm(x.astype(BF16), (eye + n4).astype(BF16))
    size = 8
    while size < c:
        lower_left = ((_block_id(row, size) == _block_id(col, size) + 1)
                      & (_block_id(row, 2 * size) == _block_id(col, 2 * size)))
        cm = jnp.where(lower_left, m_strict, 0.0).astype(BF16)
        xb = x.astype(BF16)
        x = x - _bmm(_bmm(xb, cm).astype(BF16), xb)
        size *= 2
    return x


def _gdn_kernel(q_ref, k_ref, v_ref, z_ref, beta_ref, gc_ref, gl_ref, gct_ref, nw_ref, o_ref, state_ref,
                *, n_chunks, nkh):
    gi = pl.program_id(1)
    si = pl.program_id(2)
    c = GDN_CHUNK
    t = n_chunks * c
    nvh = 2 * nkh

    @pl.when(si == 0)
    def _():
        state_ref[...] = jnp.zeros_like(state_ref)

    lane = lax.broadcasted_iota(jnp.int32, (t, LANES), 1)
    row = lax.broadcasted_iota(jnp.int32, (c, c), 0)
    col = lax.broadcasted_iota(jnp.int32, (c, c), 1)
    strict = col < row
    incl = col <= row

    nc = n_chunks

    def column(ref, head):
        col_vec = jnp.sum(jnp.where(lane == head, ref[0], 0.0), axis=1, keepdims=True)
        return col_vec.reshape(nc, c, 1)

    def heads(per_head):
        return jnp.concatenate([per_head(vh) for vh in range(nvh)], axis=0)

    def head_cols(ref, vh):
        return ref[0, :, vh * HEAD_DIM:(vh + 1) * HEAD_DIM].reshape(nc, c, HEAD_DIM)

    beta = heads(lambda vh: column(beta_ref, gi * nvh + vh))
    gcum = heads(lambda vh: column(gc_ref, gi * nvh + vh))
    glast = heads(lambda vh: column(gl_ref, gi * nvh + vh))
    g_row = heads(lambda vh: gct_ref[0, vh // 2, :, vh % 2:vh % 2 + 1, :])
    q = [head_cols(q_ref, kh) for kh in range(nkh)]
    k = [head_cols(k_ref, kh) for kh in range(nkh)]
    kk = [_bmm_nt(k[kh], k[kh]) for kh in range(nkh)]
    qk = [_bmm_nt(q[kh], k[kh]) for kh in range(nkh)]
    qf = heads(lambda vh: q[vh // 2].astype(F32))
    kf = heads(lambda vh: k[vh // 2].astype(F32))
    v = heads(lambda vh: head_cols(v_ref, vh)).astype(F32)

    decay = jnp.exp(jnp.minimum(gcum - g_row, 0.0))
    m_strict = jnp.where(strict, heads(lambda vh: kk[vh // 2]) * beta * decay, 0.0)
    attn = jnp.where(incl, heads(lambda vh: qk[vh // 2]) * decay, 0.0).astype(BF16)
    t_inv = _inv_unit_lower(m_strict, row, col).astype(BF16)
    u = _bmm(t_inv, (v * beta).astype(BF16))
    w = _bmm(t_inv, (kf * (beta * jnp.exp(gcum))).astype(BF16)).astype(BF16)
    q_dec = (qf * jnp.exp(gcum)).astype(BF16)
    k_dec = (kf * jnp.exp(glast - gcum)).astype(BF16)
    chunk_decay = jnp.exp(glast[:, 0:1, :])
    nw = nw_ref[...]

    state = state_ref[...]
    for ch in range(nc):
        def pick(x):
            return jnp.concatenate([x[vh * nc + ch:vh * nc + ch + 1] for vh in range(nvh)], axis=0)

        state_b = state.astype(BF16)
        v_new = pick(u) - _bmm(pick(w), state_b)
        v_new_b = v_new.astype(BF16)
        o = _bmm(pick(q_dec), state_b) + _bmm(pick(attn), v_new_b)
        state = state * pick(chunk_decay) + _bmm_tn(pick(k_dec), v_new_b)
        ms = jnp.mean(o * o, axis=2, keepdims=True)
        rows = slice(ch * c, (ch + 1) * c)
        zf = z_ref[0, rows, :].astype(F32)
        for vh in range(nvh):
            hs = slice(vh * HEAD_DIM, (vh + 1) * HEAD_DIM)
            o_ref[0, rows, hs] = (o[vh] * lax.rsqrt(ms[vh] + EPS) * nw * _silu(zf[:, hs])).astype(o_ref.dtype)
    state_ref[...] = state


def gdn_core(qkv_act, proj, beta, gc, gl, gct, norm_w, n_k_heads, n_chunks=4, nkh=2):
    b, s, _ = qkv_act.shape
    nkh = min(nkh, n_k_heads)
    assert n_k_heads % nkh == 0
    g = n_k_heads // nkh
    kw, vw = nkh * HEAD_DIM, 2 * nkh * HEAD_DIM
    n_chunks = min(n_chunks, s // GDN_CHUNK)
    t = n_chunks * GDN_CHUNK
    gate = pl.BlockSpec((1, t, LANES), lambda bi, gi, si: (bi, si, 0))
    return pl.pallas_call(
        functools.partial(_gdn_kernel, n_chunks=n_chunks, nkh=nkh),
        grid=(b, g, s // t),
        in_specs=[pl.BlockSpec((1, t, kw), lambda bi, gi, si: (bi, si, gi)),
                  pl.BlockSpec((1, t, kw), lambda bi, gi, si: (bi, si, g + gi)),
                  pl.BlockSpec((1, t, vw), lambda bi, gi, si: (bi, si, g + gi)),
                  pl.BlockSpec((1, t, vw), lambda bi, gi, si: (bi, si, 2 * g + gi)),
                  gate, gate, gate,
                  pl.BlockSpec((1, nkh, n_chunks, 2, GDN_CHUNK), lambda bi, gi, si: (bi, gi, si, 0, 0)),
                  pl.BlockSpec((1, HEAD_DIM), lambda bi, gi, si: (0, 0))],
        out_specs=pl.BlockSpec((1, t, vw), lambda bi, gi, si: (bi, si, gi)),
        out_shape=jax.ShapeDtypeStruct((b, s, 2 * n_k_heads * HEAD_DIM), BF16),
        scratch_shapes=[pltpu.VMEM((2 * nkh, HEAD_DIM, HEAD_DIM), F32)],
        compiler_params=_params("parallel", "parallel", "arbitrary"),
        name="gdn_core",
    )(qkv_act, qkv_act, qkv_act, proj, beta, gc, gl, gct, norm_w)


def _router_kernel(logit_ref, idx_ref, w_ref, *, n_experts):
    x = logit_ref[...]
    lane = lax.broadcasted_iota(jnp.int32, x.shape, 1)
    x = jnp.where(lane < n_experts, x, -jnp.inf)
    m1 = jnp.max(x, axis=1, keepdims=True)
    i1 = jnp.min(jnp.where(x == m1, lane, LANES), axis=1, keepdims=True)
    x2 = jnp.where(lane == i1, -jnp.inf, x)
    m2 = jnp.max(x2, axis=1, keepdims=True)
    i2 = jnp.min(jnp.where(x2 == m2, lane, LANES), axis=1, keepdims=True)
    e2 = jnp.exp(m2 - m1)
    w1 = 1.0 / (1.0 + e2)
    w2 = e2 / (1.0 + e2)
    idx_ref[...] = jnp.where(lane == 0, i1, jnp.where(lane == 1, i2, 0))
    w_ref[...] = jnp.where(lane == 0, w1, jnp.where(lane == 1, w2, 0.0))


def router_top2(logits, n_experts, tm=1024):
    m, n = logits.shape
    spec = pl.BlockSpec((tm, n), lambda i: (i, 0))
    return pl.pallas_call(
        functools.partial(_router_kernel, n_experts=n_experts),
        grid=(m // tm,),
        in_specs=[spec],
        out_specs=[spec, spec],
        out_shape=[jax.ShapeDtypeStruct((m, n), jnp.int32), jax.ShapeDtypeStruct((m, n), F32)],
        compiler_params=_params("parallel"),
        name="router",
    )(logits)


MOE_ROW_TILE = 256


def _route(idx, n_experts, tm):
    m = idx.shape[0]
    a = 2 * m
    e = idx.reshape(a)
    onehot = (e[:, None] == jnp.arange(n_experts, dtype=jnp.int32)[None, :]).astype(jnp.int32)
    csum = jnp.cumsum(onehot, axis=0)
    counts = csum[-1]
    padded = (counts + tm - 1) // tm * tm
    ends = jnp.cumsum(padded)
    pos = jnp.sum(onehot * (csum - 1 + (ends - padded)[None, :]), axis=1)
    n_tiles = a // tm + n_experts
    tile_start = jnp.arange(n_tiles, dtype=jnp.int32) * tm
    tile_expert = jnp.minimum(jnp.sum((tile_start[:, None] >= ends[None, :]).astype(jnp.int32), axis=1),
                              n_experts - 1)
    n_used = (ends[-1] // tm).astype(jnp.int32).reshape(1)
    row_token = jnp.zeros((n_tiles * tm,), jnp.int32).at[pos].set(jnp.arange(a, dtype=jnp.int32) // 2)
    return pos.astype(jnp.int32), tile_expert, n_used, row_token


DMA_ISSUE_UNROLL = 8


def _row_copy(src_hbm, src_row, dst_vmem, dst_row, sem):
    return pltpu.make_async_copy(src_hbm.at[pl.ds(src_row, 1)], dst_vmem.at[pl.ds(dst_row, 1)], sem)


def _rows_wait(src_hbm, dst_vmem, sem):
    pltpu.make_async_copy(src_hbm.at[pl.ds(0, dst_vmem.shape[0])], dst_vmem, sem).wait()


def _moe_gather_kernel(row_token_ref, n_used_ref, h_hbm, g_ref, o_ref, buf, sem, *, tm):
    i = pl.program_id(0)
    n_used = n_used_ref[0]

    def issue(tile):
        slot = lax.rem(tile, 2)

        def start(r, c):
            _row_copy(h_hbm, row_token_ref[tile * tm + r], buf.at[slot], r, sem.at[slot]).start()
            return c

        lax.fori_loop(0, tm, start, 0, unroll=DMA_ISSUE_UNROLL)

    @pl.when(i == 0)
    def _():
        issue(i)

    @pl.when(i + 1 < n_used)
    def _():
        issue(i + 1)

    @pl.when(i < n_used)
    def _():
        slot = lax.rem(i, 2)
        _rows_wait(h_hbm, buf.at[slot], sem.at[slot])
        x = buf[slot]
        ms = jnp.mean(x * x, axis=-1, keepdims=True)
        o_ref[...] = (x * lax.rsqrt(ms + EPS) * g_ref[...]).astype(o_ref.dtype)

    @pl.when(i >= n_used)
    def _():
        o_ref[...] = jnp.zeros_like(o_ref)


def moe_gather(h, g, row_token, n_used, tm):
    m, d = h.shape
    p = row_token.shape[0]
    return pl.pallas_call(
        functools.partial(_moe_gather_kernel, tm=tm),
        grid_spec=pltpu.PrefetchScalarGridSpec(
            num_scalar_prefetch=2,
            grid=(p // tm,),
            in_specs=[pl.BlockSpec(memory_space=pl.ANY),
                      pl.BlockSpec((1, d), lambda i, rt, nu: (0, 0))],
            out_specs=pl.BlockSpec((tm, d), lambda i, rt, nu: (i, 0)),
            scratch_shapes=[pltpu.VMEM((2, tm, d), F32), pltpu.SemaphoreType.DMA((2,))]),
        out_shape=jax.ShapeDtypeStruct((p, d), BF16),
        compiler_params=_params("arbitrary"),
        name="moe_gather",
    )(row_token, n_used, h, g.reshape(1, d))


def _moe_up_kernel(te_ref, n_used_ref, x_ref, wg_ref, wu_ref, o_ref):
    i = pl.program_id(1)

    @pl.when(i < n_used_ref[0])
    def _():
        x = x_ref[...]
        gate = _dot(x, wg_ref[...])
        up = _dot(x, wu_ref[...])
        o_ref[...] = (_silu(gate) * up).astype(o_ref.dtype)

    @pl.when(i >= n_used_ref[0])
    def _():
        o_ref[...] = jnp.zeros_like(o_ref)


def moe_up(xs, w_gate_up, layer, tile_expert, n_used, tm, tn=1408):
    p, d = xs.shape
    f = w_gate_up.shape[3] // 2
    tn = _tile(f, tn)
    nj = f // tn

    def row_tile(j, i, te, nu):
        return (jnp.minimum(i, nu[0] - 1), 0)

    return pl.pallas_call(
        _moe_up_kernel,
        grid_spec=pltpu.PrefetchScalarGridSpec(
            num_scalar_prefetch=2,
            grid=(nj, p // tm),
            in_specs=[pl.BlockSpec((tm, d), row_tile),
                      pl.BlockSpec((None, None, d, tn), lambda j, i, te, nu: (layer, te[i], 0, j)),
                      pl.BlockSpec((None, None, d, tn), lambda j, i, te, nu: (layer, te[i], 0, j + nj))],
            out_specs=pl.BlockSpec((tm, tn), lambda j, i, te, nu: (i, j))),
        out_shape=jax.ShapeDtypeStruct((p, f), BF16),
        compiler_params=_params("parallel", "arbitrary"),
        name="moe_up",
    )(tile_expert, n_used, xs, w_gate_up, w_gate_up)


def _moe_down_kernel(te_ref, n_used_ref, x_ref, w_ref, o_ref):
    i = pl.program_id(0)

    @pl.when(i < n_used_ref[0])
    def _():
        o_ref[...] = _dot(x_ref[...], w_ref[...])

    @pl.when(i >= n_used_ref[0])
    def _():
        o_ref[...] = jnp.zeros_like(o_ref)


def moe_down(hs, w_down, layer, tile_expert, n_used, tm):
    p, f = hs.shape
    d = w_down.shape[3]
    return pl.pallas_call(
        _moe_down_kernel,
        grid_spec=pltpu.PrefetchScalarGridSpec(
            num_scalar_prefetch=2,
            grid=(p // tm,),
            in_specs=[pl.BlockSpec((tm, f), lambda i, te, nu: (jnp.minimum(i, nu[0] - 1), 0)),
                      pl.BlockSpec((None, None, f, d), lambda i, te, nu: (layer, te[i], 0, 0))],
            out_specs=pl.BlockSpec((tm, d), lambda i, te, nu: (i, 0))),
        out_shape=jax.ShapeDtypeStruct((p, d), F32),
        compiler_params=_params("arbitrary"),
        name="moe_down",
    )(tile_expert, n_used, hs, w_down)


def _moe_combine_kernel(pos_ref, h_ref, w_ref, g_ref, y_hbm, *refs, tm, emit_h):
    if emit_h:
        o_ref, n_ref, buf, sem = refs
    else:
        n_ref, buf, sem = refs
    i = pl.program_id(0)

    def issue(tile):
        slot = lax.rem(tile, 2)

        def start(r, c):
            for k in range(2):
                _row_copy(y_hbm, pos_ref[2 * (tile * tm + r) + k], buf.at[slot, k], r, sem.at[slot]).start()
            return c

        lax.fori_loop(0, tm, start, 0, unroll=DMA_ISSUE_UNROLL)

    @pl.when(i == 0)
    def _():
        issue(i)

    @pl.when(i + 1 < pl.num_programs(0))
    def _():
        issue(i + 1)

    slot = lax.rem(i, 2)
    for k in range(2):
        _rows_wait(y_hbm, buf.at[slot, k], sem.at[slot])
    w = w_ref[...]
    x = h_ref[...] + w[:, 0:1] * buf[slot, 0] + w[:, 1:2] * buf[slot, 1]
    if emit_h:
        o_ref[...] = x
    ms = jnp.mean(x * x, axis=-1, keepdims=True)
    n_ref[...] = (x * lax.rsqrt(ms + EPS) * g_ref[...]).astype(n_ref.dtype)


def moe_combine(h, top_w, ys, pos, g_next, norm_dtype, emit_h, tm=256):
    m, d = h.shape
    row_block = pl.BlockSpec((tm, d), lambda i, pos: (i, 0))
    out_specs = [row_block, row_block] if emit_h else [row_block]
    out_shape = [jax.ShapeDtypeStruct((m, d), F32)] if emit_h else []
    out_shape.append(jax.ShapeDtypeStruct((m, d), norm_dtype))
    outs = pl.pallas_call(
        functools.partial(_moe_combine_kernel, tm=tm, emit_h=emit_h),
        grid_spec=pltpu.PrefetchScalarGridSpec(
            num_scalar_prefetch=1,
            grid=(m // tm,),
            in_specs=[row_block,
                      pl.BlockSpec((tm, LANES), lambda i, pos: (i, 0)),
                      pl.BlockSpec((1, d), lambda i, pos: (0, 0)),
                      pl.BlockSpec(memory_space=pl.ANY)],
            out_specs=out_specs,
            scratch_shapes=[pltpu.VMEM((2, 2, tm, d), F32), pltpu.SemaphoreType.DMA((2,))]),
        out_shape=out_shape,
        compiler_params=_params("arbitrary"),
        name="moe_combine",
    )(pos, h, top_w, g_next.reshape(1, d), ys)
    return (outs[0], outs[1]) if emit_h else (None, outs[0])


def _sb_layer(h, hn, w_in, w_out, layer):
    b, s, d = h.shape
    n_heads = w_out.shape[1] // HEAD_DIM
    qkv = matmul(hn, w_in, layer, w_in.shape[2], BF16)
    o = sb_attention(qkv.reshape(b, s, -1), n_heads)
    return matmul(o.reshape(b * s, -1), w_out, layer, d, F32, res=h.reshape(b * s, d)).reshape(b, s, d)


def _fox_layer(h, hn, g, w_in, b_f, w_out, layer):
    b, s, d = h.shape
    n_heads = w_out.shape[1] // HEAD_DIM
    main = 4 * n_heads * HEAD_DIM
    h2 = h.reshape(b * s, d)
    proj = matmul(hn, w_in, layer, main, BF16).reshape(b, s, main)
    f_raw = small_proj(h2, g, _pad_cols(w_in[layer, :, main:], LANES)).reshape(b, s, LANES)
    bias = jnp.pad(b_f, (0, LANES - n_heads)).reshape(1, LANES)
    cum_col, cum_t = fox_gate(f_raw, bias)
    o = fox_attention(proj, cum_col, cum_t[:, :n_heads, :], n_heads)
    return matmul(o.reshape(b * s, -1), w_out, layer, d, F32, res=h2).reshape(b, s, d)


def _gdn_layer(h, hn, g, w_in, conv_w, a_log, dt_bias, norm_w, w_out, layer):
    b, s, d = h.shape
    v_dim = w_out.shape[1]
    n_v_heads = v_dim // HEAD_DIM
    n_k_heads = n_v_heads // 2
    k_dim = n_k_heads * HEAD_DIM
    qkv_dim = 2 * k_dim + v_dim
    main = qkv_dim + v_dim
    h2 = h.reshape(b * s, d)
    proj = matmul(hn, w_in, layer, main, BF16).reshape(b, s, main)
    w_gates = jnp.concatenate([_pad_cols(w_in[layer, :, main:main + n_v_heads], LANES),
                               _pad_cols(w_in[layer, :, main + n_v_heads:], LANES)], axis=1)
    raw = small_proj(h2, g, w_gates).reshape(b, s, 2 * LANES)
    pad = (0, LANES - n_v_heads)
    beta, gc, gl, gct = gdn_gates(raw[..., :LANES], raw[..., LANES:],
                                  jnp.pad(a_log, pad).reshape(1, LANES), jnp.pad(dt_bias, pad).reshape(1, LANES))
    gct = gct[:, :n_v_heads, :].reshape(b, n_k_heads, 2, s // GDN_CHUNK, GDN_CHUNK).transpose(0, 1, 3, 2, 4)
    qkv_act = gdn_prep(proj, conv_w.T, qkv_dim, k_dim)
    o = gdn_core(qkv_act, proj, beta, gc, gl, gct, norm_w.reshape(1, HEAD_DIM), n_k_heads)
    return matmul(o.reshape(b * s, v_dim), w_out, layer, d, F32, res=h2).reshape(b, s, d)


def _dense_ffn(h, g, w_gate_up, w_down, layer):
    b, s, d = h.shape
    h2 = h.reshape(b * s, d)
    hn = rmsnorm(h2, g, BF16)
    mid = swiglu_up(hn, w_gate_up, layer)
    return matmul(mid, w_down, layer, d, F32, res=h2).reshape(b, s, d)


def _moe_ffn(h, g, w_router, w_gate_up, w_down, layer, g_next, norm_dtype, emit_h):
    b, s, d = h.shape
    n_experts = w_router.shape[1]
    h2 = h.reshape(b * s, d)
    logits = small_proj(h2, g, _pad_cols(w_router, LANES))
    top_idx, top_w = router_top2(logits, n_experts)
    tm = MOE_ROW_TILE
    pos, tile_expert, n_used, row_token = _route(top_idx[:, :2], n_experts, tm)
    xs = moe_gather(h2, g, row_token, n_used, tm)
    mid = moe_up(xs, w_gate_up, layer, tile_expert, n_used, tm)
    ys = moe_down(mid, w_down, layer, tile_expert, n_used, tm)
    new_h, normed = moe_combine(h2, top_w, ys, pos, g_next, norm_dtype, emit_h)
    return (new_h.reshape(b, s, d) if emit_h else None), normed


def kernel(x, norm_mix, norm_ffn, sb_w_in, sb_w_out, gdn_w_in, gdn_conv_w, gdn_a_log, gdn_dt_bias, gdn_norm_w,
           gdn_w_out, fox_w_in, fox_b_f, fox_w_out, ffn_w_gate_up, ffn_w_down, moe_w_router, moe_w_gate_up,
           moe_w_down, final_norm):
    depth = norm_mix.shape[0]
    b, s, d = x.shape
    moe_gate_up_b = moe_w_gate_up.astype(BF16)
    moe_down_b = moe_w_down.astype(BF16)
    h = x
    hn = None
    for i in range(depth):
        kind, j = i % 3, i // 3
        if hn is None:
            hn = rmsnorm(h.reshape(b * s, d), norm_mix[i], BF16)
        if kind == 0:
            h = _sb_layer(h, hn, sb_w_in, sb_w_out, j)
        elif kind == 1:
            h = _gdn_layer(h, hn, norm_mix[i], gdn_w_in, gdn_conv_w[j], gdn_a_log[j], gdn_dt_bias[j],
                           gdn_norm_w[j], gdn_w_out, j)
        else:
            h = _fox_layer(h, hn, norm_mix[i], fox_w_in, fox_b_f[j], fox_w_out, j)
        hn = None
        f = i // 2
        last = i == depth - 1
        if i % 2 == 0:
            h = _dense_ffn(h, norm_ffn[i], ffn_w_gate_up, ffn_w_down, f)
        elif last:
            _, out = _moe_ffn(h, norm_ffn[i], moe_w_router[f], moe_gate_up_b, moe_down_b, f,
                              final_norm, x.dtype, emit_h=False)
            return out.reshape(b, s, d)
        else:
            h, hn = _moe_ffn(h, norm_ffn[i], moe_w_router[f], moe_gate_up_b, moe_down_b, f,
                             norm_mix[i + 1], BF16, emit_h=True)
    return rmsnorm(h.reshape(b * s, d), final_norm, x.dtype).reshape(b, s, d)
```

```python
import functools

import jax
import jax.numpy as jnp
from jax import lax
from jax.experimental import pallas as pl
from jax.experimental.pallas import tpu as pltpu

HEAD_DIM = 128
LANES = 128
GDN_CONV = 4
GDN_CHUNK = 128
EPS = 1e-6
LOG2E = 1.4426950408889634
EXP_UNDERFLOW = -104.0
VMEM_LIMIT_BYTES = 56 * 1024 * 1024

F32 = jnp.float32
BF16 = jnp.bfloat16


def _params(*semantics):
    return pltpu.CompilerParams(dimension_semantics=semantics, vmem_limit_bytes=VMEM_LIMIT_BYTES)


def _tile(dim, preferred):
    assert dim % LANES == 0, dim
    t = min(preferred, dim) // LANES * LANES
    while dim % t:
        t -= LANES
    return t


def _dot(a, b):
    return jnp.dot(a, b, preferred_element_type=F32)


def _split3(x):
    hi = x.astype(BF16)
    r = x - hi.astype(F32)
    mid = r.astype(BF16)
    lo = (r - mid.astype(F32)).astype(BF16)
    return hi, mid, lo


def _split2(x):
    hi = x.astype(BF16)
    lo = (x - hi.astype(F32)).astype(BF16)
    return hi, lo


def _softplus(z):
    return jnp.maximum(z, 0.0) + jnp.log(1.0 + jnp.exp(-jnp.abs(z)))


def _sigmoid(z):
    return 1.0 / (1.0 + jnp.exp(-z))


def _silu(z):
    return z * _sigmoid(z)


def _block_id(idx, size):
    shift = size.bit_length() - 1
    assert 1 << shift == size
    return jnp.right_shift(idx, shift)


def _rmsnorm_kernel(x_ref, g_ref, o_ref):
    x = x_ref[...]
    ms = jnp.mean(x * x, axis=-1, keepdims=True)
    o_ref[...] = (x * lax.rsqrt(ms + EPS) * g_ref[...]).astype(o_ref.dtype)


def rmsnorm(x, g, out_dtype, tm=512):
    m, d = x.shape
    return pl.pallas_call(
        _rmsnorm_kernel,
        grid=(m // tm,),
        in_specs=[pl.BlockSpec((tm, d), lambda i: (i, 0)),
                  pl.BlockSpec((1, d), lambda i: (0, 0))],
        out_specs=pl.BlockSpec((tm, d), lambda i: (i, 0)),
        out_shape=jax.ShapeDtypeStruct((m, d), out_dtype),
        compiler_params=_params("parallel"),
        name="rmsnorm",
    )(x, g.reshape(1, d))


WS_VMEM_BUDGET = 44 * 1024 * 1024


def _ws_tiles(m, kdim, n, n_weight_blocks, out_bytes, has_res):
    for tm_pref, tn_pref in ((1024, 1024), (1024, 512), (512, 512), (512, 256), (256, 256), (256, 128)):
        tm, tn = _tile(m, tm_pref), _tile(n, tn_pref)
        weights = n_weight_blocks * kdim * tn * (2 * 4 + 2)
        acts = 2 * tm * kdim * 2
        outs = 2 * tm * tn * (out_bytes + (4 if has_res else 0))
        results = n_weight_blocks * tm * tn * 4
        if weights + acts + outs + results <= WS_VMEM_BUDGET:
            return tm, tn
    raise ValueError(f"no weight-stationary tiling fits VMEM for {(m, kdim, n)}")


def _matmul_kernel(*refs, has_res):
    if has_res:
        a_ref, w_ref, res_ref, o_ref, wb_ref = refs
    else:
        a_ref, w_ref, o_ref, wb_ref = refs

    @pl.when(pl.program_id(1) == 0)
    def _():
        wb_ref[...] = w_ref[...].astype(BF16)

    r = _dot(a_ref[...], wb_ref[...])
    if has_res:
        r = r + res_ref[...]
    o_ref[...] = r.astype(o_ref.dtype)


def matmul(a, w, layer, n_cols, out_dtype, res=None):
    m, kdim = a.shape
    tm, tn = _ws_tiles(m, kdim, n_cols, 1, jnp.dtype(out_dtype).itemsize, res is not None)
    in_specs = [pl.BlockSpec((tm, kdim), lambda j, i: (i, 0)),
                pl.BlockSpec((kdim, tn), lambda j, i: (layer, j))]
    args = [a, w.reshape(-1, w.shape[2])]
    if res is not None:
        in_specs.append(pl.BlockSpec((tm, tn), lambda j, i: (i, j)))
        args.append(res)
    return pl.pallas_call(
        functools.partial(_matmul_kernel, has_res=res is not None),
        grid=(n_cols // tn, m // tm),
        in_specs=in_specs,
        out_specs=pl.BlockSpec((tm, tn), lambda j, i: (i, j)),
        out_shape=jax.ShapeDtypeStruct((m, n_cols), out_dtype),
        scratch_shapes=[pltpu.VMEM((kdim, tn), BF16)],
        compiler_params=_params("parallel", "arbitrary"),
        name="matmul",
    )(*args)


def _swiglu_up_kernel(a_ref, wg_ref, wu_ref, o_ref, wgb_ref, wub_ref):
    @pl.when(pl.program_id(1) == 0)
    def _():
        wgb_ref[...] = wg_ref[...].astype(BF16)
        wub_ref[...] = wu_ref[...].astype(BF16)

    a = a_ref[...]
    o_ref[...] = (_silu(_dot(a, wgb_ref[...])) * _dot(a, wub_ref[...])).astype(o_ref.dtype)


def swiglu_up(a, w_gate_up, layer):
    m, kdim = a.shape
    f = w_gate_up.shape[2] // 2
    tm, tn = _ws_tiles(m, kdim, f, 2, 2, False)
    nj = f // tn
    return pl.pallas_call(
        _swiglu_up_kernel,
        grid=(nj, m // tm),
        in_specs=[pl.BlockSpec((tm, kdim), lambda j, i: (i, 0)),
                  pl.BlockSpec((None, kdim, tn), lambda j, i: (layer, 0, j)),
                  pl.BlockSpec((None, kdim, tn), lambda j, i: (layer, 0, j + nj))],
        out_specs=pl.BlockSpec((tm, tn), lambda j, i: (i, j)),
        out_shape=jax.ShapeDtypeStruct((m, f), BF16),
        scratch_shapes=[pltpu.VMEM((kdim, tn), BF16), pltpu.VMEM((kdim, tn), BF16)],
        compiler_params=_params("parallel", "arbitrary"),
        name="swiglu_up",
    )(a, w_gate_up, w_gate_up)


def _small_proj_kernel(h_ref, g_ref, w_ref, o_ref):
    x = h_ref[...]
    ms = jnp.mean(x * x, axis=-1, keepdims=True)
    xn = x * lax.rsqrt(ms + EPS) * g_ref[...]
    xh, xl = _split2(xn)
    wh, wl = _split2(w_ref[...])
    o_ref[...] = _dot(xh, wh) + _dot(xl, wh) + _dot(xh, wl)


def small_proj(h, g, w, tm=512):
    m, d = h.shape
    n = w.shape[1]
    return pl.pallas_call(
        _small_proj_kernel,
        grid=(m // tm,),
        in_specs=[pl.BlockSpec((tm, d), lambda i: (i, 0)),
                  pl.BlockSpec((1, d), lambda i: (0, 0)),
                  pl.BlockSpec((d, n), lambda i: (0, 0))],
        out_specs=pl.BlockSpec((tm, n), lambda i: (i, 0)),
        out_shape=jax.ShapeDtypeStruct((m, n), F32),
        compiler_params=_params("parallel"),
        name="small_proj",
    )(h, g.reshape(1, d), w)


def _pad_cols(w, width):
    return jnp.pad(w, ((0, 0), (0, width - w.shape[1])))


def _split_heads(x, n):
    return jnp.stack([x[:, h * HEAD_DIM:(h + 1) * HEAD_DIM] for h in range(n)], axis=0)


def _sb_attn_kernel(q_ref, k_ref, v_ref, o_ref, *, t, hb, scale):
    i = pl.program_id(2)
    q = _split_heads(q_ref[0], hb)
    row = lax.broadcasted_iota(jnp.int32, (t, t), 0)
    col = lax.broadcasted_iota(jnp.int32, (t, t), 1)
    strict = col < row
    suffix = jnp.where(row > col, 1.0, 0.0).astype(BF16)

    def block(kb, carry, acc, masked):
        start = pl.multiple_of(kb * t, t)
        k = _split_heads(k_ref[0, pl.ds(start, t), :], hb)
        v = _split_heads(v_ref[0, pl.ds(start, t), :], hb)
        z = _bmm_nt(q, k) * scale
        sp = _softplus(z)
        log_keep = -sp
        if masked:
            log_keep = jnp.where(strict, log_keep, 0.0)
        hi, lo = _split2(log_keep.reshape(hb * t, t))
        later = (_dot(hi, suffix) + _dot(lo, suffix)).reshape(hb, t, t) + carry
        w = jnp.exp(z - sp + later)
        if masked:
            w = jnp.where(strict, w, 0.0)
        acc = acc + _bmm(w.astype(BF16), v)
        carry = carry + jnp.sum(log_keep, axis=2, keepdims=True)
        return carry, acc

    carry, acc = block(i, jnp.zeros((hb, t, 1), F32), jnp.zeros((hb, t, HEAD_DIM), F32), True)

    def live(c):
        return (c[0] < i) & (c[3] > EXP_UNDERFLOW)

    def body(c):
        carry, acc = block(i - 1 - c[0], c[1], c[2], False)
        return c[0] + 1, carry, acc, jnp.max(carry)

    _, carry, acc, _ = lax.while_loop(live, body, (jnp.int32(0), carry, acc, jnp.max(carry)))
    for h in range(hb):
        o_ref[0, :, h * HEAD_DIM:(h + 1) * HEAD_DIM] = acc[h].astype(o_ref.dtype)


def sb_attention(qkv, n_heads, t=256, hb=4):
    b, s, _ = qkv.shape
    hb = min(hb, n_heads)
    assert n_heads % hb == 0
    g = n_heads // hb
    w = hb * HEAD_DIM
    return pl.pallas_call(
        functools.partial(_sb_attn_kernel, t=t, hb=hb, scale=HEAD_DIM ** -0.5),
        grid=(b, g, s // t),
        in_specs=[pl.BlockSpec((1, t, w), lambda bi, gi, i: (bi, i, gi)),
                  pl.BlockSpec((1, s, w), lambda bi, gi, i: (bi, 0, g + gi)),
                  pl.BlockSpec((1, s, w), lambda bi, gi, i: (bi, 0, 2 * g + gi))],
        out_specs=pl.BlockSpec((1, t, w), lambda bi, gi, i: (bi, i, gi)),
        out_shape=jax.ShapeDtypeStruct((b, s, n_heads * HEAD_DIM), BF16),
        compiler_params=_params("parallel", "parallel", "arbitrary"),
        name="sb_attention",
    )(qkv, qkv, qkv)


def _fox_gate_kernel(f_ref, bias_ref, col_ref, row_ref, carry_ref, *, t):
    i = pl.program_id(1)

    @pl.when(i == 0)
    def _():
        carry_ref[...] = jnp.zeros_like(carry_ref)

    x = f_ref[0] + bias_ref[...]
    log_f = -_softplus(-x)
    row = lax.broadcasted_iota(jnp.int32, (t, t), 0)
    col = lax.broadcasted_iota(jnp.int32, (t, t), 1)
    lower = jnp.where(col <= row, 1.0, 0.0).astype(BF16)
    hi, mid, lo = _split3(log_f)
    cum = _dot(lower, hi) + _dot(lower, mid) + _dot(lower, lo) + carry_ref[...]
    col_ref[0] = cum
    row_ref[0] = cum.T
    carry_ref[...] = cum[t - 1:t, :]


def fox_gate(f_raw, bias, t=512):
    b, s, n = f_raw.shape
    return pl.pallas_call(
        functools.partial(_fox_gate_kernel, t=t),
        grid=(b, s // t),
        in_specs=[pl.BlockSpec((1, t, n), lambda bi, i: (bi, i, 0)),
                  pl.BlockSpec((1, n), lambda bi, i: (0, 0))],
        out_specs=[pl.BlockSpec((1, t, n), lambda bi, i: (bi, i, 0)),
                   pl.BlockSpec((1, n, t), lambda bi, i: (bi, 0, i))],
        out_shape=[jax.ShapeDtypeStruct((b, s, n), F32), jax.ShapeDtypeStruct((b, n, s), F32)],
        scratch_shapes=[pltpu.VMEM((1, n), F32)],
        compiler_params=_params("parallel", "arbitrary"),
        name="fox_gate",
    )(f_raw, bias)


def _fox_attn_kernel(q_ref, k_ref, v_ref, gate_ref, cq_ref, ck_ref, o_ref, *, t, hb, scale):
    gi = pl.program_id(1)
    i = pl.program_id(2)
    q = _split_heads(q_ref[0], hb)
    lane = lax.broadcasted_iota(jnp.int32, (t, LANES), 1)
    cq = cq_ref[0]
    cum_q = jnp.stack([jnp.sum(jnp.where(lane == gi * hb + h, cq, 0.0), axis=1, keepdims=True)
                       for h in range(hb)], axis=0) * LOG2E
    row = lax.broadcasted_iota(jnp.int32, (t, t), 0)
    col = lax.broadcasted_iota(jnp.int32, (t, t), 1)
    causal = col <= row

    def block(kb, m, l, acc, masked):
        start = pl.multiple_of(kb * t, t)
        k = _split_heads(k_ref[0, pl.ds(start, t), :], hb)
        v = _split_heads(v_ref[0, pl.ds(start, t), :], hb)
        cum_k = ck_ref[0, :, pl.ds(kb, 1), :] * LOG2E
        logits = _bmm_nt(q, k) * (scale * LOG2E) + cum_q - cum_k
        if masked:
            logits = jnp.where(causal, logits, -jnp.inf)
        m_new = jnp.maximum(m, jnp.max(logits, axis=2, keepdims=True))
        alpha = jnp.exp2(m - m_new)
        p = jnp.exp2(logits - m_new)
        l = l * alpha + jnp.sum(p, axis=2, keepdims=True)
        acc = acc * alpha + _bmm(p.astype(BF16), v)
        return m_new, l, acc

    init = (jnp.full((hb, t, 1), -jnp.inf, F32), jnp.zeros((hb, t, 1), F32), jnp.zeros((hb, t, HEAD_DIM), F32))
    m, l, acc = block(i, *init, True)

    def body(n, c):
        return block(i - 1 - n, *c, False)

    m, l, acc = lax.fori_loop(0, i, body, (m, l, acc))
    out = acc / l
    for h in range(hb):
        hs = slice(h * HEAD_DIM, (h + 1) * HEAD_DIM)
        o_ref[0, :, hs] = (out[h] * _sigmoid(gate_ref[0, :, hs].astype(F32))).astype(o_ref.dtype)


def fox_attention(proj, cum_col, cum_row, n_heads, t=512, hb=4):
    b, s, _ = proj.shape
    hb = min(hb, n_heads)
    assert n_heads % hb == 0
    g = n_heads // hb
    w = hb * HEAD_DIM
    cum_row = cum_row.reshape(b, n_heads, s // t, t)
    return pl.pallas_call(
        functools.partial(_fox_attn_kernel, t=t, hb=hb, scale=HEAD_DIM ** -0.5),
        grid=(b, g, s // t),
        in_specs=[pl.BlockSpec((1, t, w), lambda bi, gi, i: (bi, i, gi)),
                  pl.BlockSpec((1, s, w), lambda bi, gi, i: (bi, 0, g + gi)),
                  pl.BlockSpec((1, s, w), lambda bi, gi, i: (bi, 0, 2 * g + gi)),
                  pl.BlockSpec((1, t, w), lambda bi, gi, i: (bi, i, 3 * g + gi)),
                  pl.BlockSpec((1, t, LANES), lambda bi, gi, i: (bi, i, 0)),
                  pl.BlockSpec((1, hb, s // t, t), lambda bi, gi, i: (bi, gi, 0, 0))],
        out_specs=pl.BlockSpec((1, t, w), lambda bi, gi, i: (bi, i, gi)),
        out_shape=jax.ShapeDtypeStruct((b, s, n_heads * HEAD_DIM), BF16),
        compiler_params=_params("parallel", "parallel", "arbitrary"),
        name="fox_attention",
    )(proj, proj, proj, proj, cum_col, cum_row)


def _gdn_gate_kernel(b_ref, a_ref, alog_ref, dt_ref, beta_ref, gc_ref, gl_ref, gct_ref, *, t, chunk):
    beta_ref[0] = _sigmoid(b_ref[0])
    g = -jnp.exp(alog_ref[...]) * _softplus(a_ref[0] + dt_ref[...])
    row = lax.broadcasted_iota(jnp.int32, (t, t), 0)
    col = lax.broadcasted_iota(jnp.int32, (t, t), 1)
    same = _block_id(row, chunk) == _block_id(col, chunk)
    lower = jnp.where(same & (col <= row), 1.0, 0.0).astype(BF16)
    whole = jnp.where(same, 1.0, 0.0).astype(BF16)
    hi, mid, lo = _split3(g)
    gc = _dot(lower, hi) + _dot(lower, mid) + _dot(lower, lo)
    gc_ref[0] = gc
    gl_ref[0] = _dot(whole, hi) + _dot(whole, mid) + _dot(whole, lo)
    gct_ref[0] = gc.T


def gdn_gates(b_raw, a_raw, a_log, dt_bias, t=512):
    b, s, n = b_raw.shape
    spec = pl.BlockSpec((1, t, n), lambda bi, i: (bi, i, 0))
    vec = pl.BlockSpec((1, n), lambda bi, i: (0, 0))
    return pl.pallas_call(
        functools.partial(_gdn_gate_kernel, t=t, chunk=GDN_CHUNK),
        grid=(b, s // t),
        in_specs=[spec, spec, vec, vec],
        out_specs=[spec, spec, spec, pl.BlockSpec((1, n, t), lambda bi, i: (bi, 0, i))],
        out_shape=[jax.ShapeDtypeStruct((b, s, n), F32)] * 3 + [jax.ShapeDtypeStruct((b, n, s), F32)],
        compiler_params=_params("parallel", "parallel"),
        name="gdn_gates",
    )(b_raw, a_raw, a_log, dt_bias)


def _gdn_prep_kernel(x_ref, prev_ref, w_ref, o_ref, *, ts, n_q_blocks, n_qk_blocks, heads_per_block):
    si = pl.program_id(1)
    ci = pl.program_id(2)
    cur = x_ref[0].astype(F32)
    prev = jnp.where(si == 0, 0.0, prev_ref[0].astype(F32))
    ext = jnp.concatenate([prev, cur], axis=0)
    w = w_ref[...]
    y = jnp.zeros_like(cur)
    for tap in range(GDN_CONV):
        off = 8 - (GDN_CONV - 1) + tap
        y = y + w[tap:tap + 1, :] * ext[off:off + ts, :]
    y = _silu(y)
    q_scale = jnp.where(ci < n_q_blocks, HEAD_DIM ** -0.5, 1.0)
    is_qk = ci < n_qk_blocks
    for hh in range(heads_per_block):
        sl = slice(hh * HEAD_DIM, (hh + 1) * HEAD_DIM)
        yh = y[:, sl]
        ss = jnp.sum(yh * yh, axis=1, keepdims=True)
        factor = jnp.where(is_qk, lax.rsqrt(ss + EPS) * q_scale, 1.0)
        o_ref[0, :, sl] = (yh * factor).astype(o_ref.dtype)


def gdn_prep(proj, conv_w_t, qkv_dim, k_dim, ts=512, tc=512):
    b, s, _ = proj.shape
    ts, tc = min(ts, s), min(tc, k_dim)
    rows8 = ts // 8
    return pl.pallas_call(
        functools.partial(_gdn_prep_kernel, ts=ts, n_q_blocks=k_dim // tc, n_qk_blocks=2 * k_dim // tc,
                          heads_per_block=tc // HEAD_DIM),
        grid=(b, s // ts, qkv_dim // tc),
        in_specs=[pl.BlockSpec((1, ts, tc), lambda bi, si, ci: (bi, si, ci)),
                  pl.BlockSpec((1, 8, tc), lambda bi, si, ci: (bi, jnp.maximum(si * rows8 - 1, 0), ci)),
                  pl.BlockSpec((GDN_CONV, tc), lambda bi, si, ci: (0, ci))],
        out_specs=pl.BlockSpec((1, ts, tc), lambda bi, si, ci: (bi, si, ci)),
        out_shape=jax.ShapeDtypeStruct((b, s, qkv_dim), BF16),
        compiler_params=_params("parallel", "parallel", "parallel"),
        name="gdn_prep",
    )(proj, proj, conv_w_t)


def _bmm(a, b):
    return lax.dot_general(a, b, (((2,), (1,)), ((0,), (0,))), preferred_element_type=F32)


def _bmm_nt(a, b):
    return lax.dot_general(a, b, (((2,), (2,)), ((0,), (0,))), preferred_element_type=F32)


def _bmm_tn(a, b):
    return lax.dot_general(a, b, (((1,), (1,)), ((0,), (0,))), preferred_element_type=F32)


def _inv_unit_lower(m_strict, row, col):
    c = m_strict.shape[-1]
    eye = jnp.where(row == col, 1.0, 0.0)
    n1 = jnp.where(_block_id(row, 8) == _block_id(col, 8), -m_strict, 0.0)
    n1b = n1.astype(BF16)
    n2 = _bmm(n1b, n1b)
    n2b = n2.astype(BF16)
    n4 = _bmm(n2b, n2b)
    x = _bmm((eye + n1).astype(BF16), (eye + n2).astype(BF16))
    x = _bmm(x.astype(BF16), (eye + n4).astype(BF16))
    size = 8
    while size < c:
        lower_left = ((_block_id(row, size) == _block_id(col, size) + 1)
                      & (_block_id(row, 2 * size) == _block_id(col, 2 * size)))
        cm = jnp.where(lower_left, m_strict, 0.0).astype(BF16)
        xb = x.astype(BF16)
        x = x - _bmm(_bmm(xb, cm).astype(BF16), xb)
        size *= 2
    return x


def _gdn_kernel(q_ref, k_ref, v_ref, z_ref, beta_ref, gc_ref, gl_ref, gct_ref, nw_ref, o_ref, state_ref,
                *, n_chunks, nkh):
    gi = pl.program_id(1)
    si = pl.program_id(2)
    c = GDN_CHUNK
    t = n_chunks * c
    nvh = 2 * nkh

    @pl.when(si == 0)
    def _():
        state_ref[...] = jnp.zeros_like(state_ref)

    lane = lax.broadcasted_iota(jnp.int32, (t, LANES), 1)
    row = lax.broadcasted_iota(jnp.int32, (c, c), 0)
    col = lax.broadcasted_iota(jnp.int32, (c, c), 1)
    strict = col < row
    incl = col <= row

    nc = n_chunks

    def column(ref, head):
        col_vec = jnp.sum(jnp.where(lane == head, ref[0], 0.0), axis=1, keepdims=True)
        return col_vec.reshape(nc, c, 1)

    def heads(per_head):
        return jnp.concatenate([per_head(vh) for vh in range(nvh)], axis=0)

    def head_cols(ref, vh):
        return ref[0, :, vh * HEAD_DIM:(vh + 1) * HEAD_DIM].reshape(nc, c, HEAD_DIM)

    beta = heads(lambda vh: column(beta_ref, gi * nvh + vh))
    gcum = heads(lambda vh: column(gc_ref, gi * nvh + vh))
    glast = heads(lambda vh: column(gl_ref, gi * nvh + vh))
    g_row = heads(lambda vh: gct_ref[0, vh // 2, :, vh % 2:vh % 2 + 1, :])
    q = [head_cols(q_ref, kh) for kh in range(nkh)]
    k = [head_cols(k_ref, kh) for kh in range(nkh)]
    kk = [_bmm_nt(k[kh], k[kh]) for kh in range(nkh)]
    qk = [_bmm_nt(q[kh], k[kh]) for kh in range(nkh)]
    qf = heads(lambda vh: q[vh // 2].astype(F32))
    kf = heads(lambda vh: k[vh // 2].astype(F32))
    v = heads(lambda vh: head_cols(v_ref, vh)).astype(F32)

    decay = jnp.exp(jnp.minimum(gcum - g_row, 0.0))
    m_strict = jnp.where(strict, heads(lambda vh: kk[vh // 2]) * beta * decay, 0.0)
    attn = jnp.where(incl, heads(lambda vh: qk[vh // 2]) * decay, 0.0).astype(BF16)
    t_inv = _inv_unit_lower(m_strict, row, col).astype(BF16)
    u = _bmm(t_inv, (v * beta).astype(BF16))
    w = _bmm(t_inv, (kf * (beta * jnp.exp(gcum))).astype(BF16)).astype(BF16)
    q_dec = (qf * jnp.exp(gcum)).astype(BF16)
    k_dec = (kf * jnp.exp(glast - gcum)).astype(BF16)
    chunk_decay = jnp.exp(glast[:, 0:1, :])
    nw = nw_ref[...]

    state = state_ref[...]
    for ch in range(nc):
        def pick(x):
            return jnp.concatenate([x[vh * nc + ch:vh * nc + ch + 1] for vh in range(nvh)], axis=0)

        state_b = state.astype(BF16)
        v_new = pick(u) - _bmm(pick(w), state_b)
        v_new_b = v_new.astype(BF16)
        o = _bmm(pick(q_dec), state_b) + _bmm(pick(attn), v_new_b)
        state = state * pick(chunk_decay) + _bmm_tn(pick(k_dec), v_new_b)
        ms = jnp.mean(o * o, axis=2, keepdims=True)
        rows = slice(ch * c, (ch + 1) * c)
        zf = z_ref[0, rows, :].astype(F32)
        for vh in range(nvh):
            hs = slice(vh * HEAD_DIM, (vh + 1) * HEAD_DIM)
            o_ref[0, rows, hs] = (o[vh] * lax.rsqrt(ms[vh] + EPS) * nw * _silu(zf[:, hs])).astype(o_ref.dtype)
    state_ref[...] = state


def gdn_core(qkv_act, proj, beta, gc, gl, gct, norm_w, n_k_heads, n_chunks=4, nkh=2):
    b, s, _ = qkv_act.shape
    nkh = min(nkh, n_k_heads)
    assert n_k_heads % nkh == 0
    g = n_k_heads // nkh
    kw, vw = nkh * HEAD_DIM, 2 * nkh * HEAD_DIM
    n_chunks = min(n_chunks, s // GDN_CHUNK)
    t = n_chunks * GDN_CHUNK
    gate = pl.BlockSpec((1, t, LANES), lambda bi, gi, si: (bi, si, 0))
    return pl.pallas_call(
        functools.partial(_gdn_kernel, n_chunks=n_chunks, nkh=nkh),
        grid=(b, g, s // t),
        in_specs=[pl.BlockSpec((1, t, kw), lambda bi, gi, si: (bi, si, gi)),
                  pl.BlockSpec((1, t, kw), lambda bi, gi, si: (bi, si, g + gi)),
                  pl.BlockSpec((1, t, vw), lambda bi, gi, si: (bi, si, g + gi)),
                  pl.BlockSpec((1, t, vw), lambda bi, gi, si: (bi, si, 2 * g + gi)),
                  gate, gate, gate,
                  pl.BlockSpec((1, nkh, n_chunks, 2, GDN_CHUNK), lambda bi, gi, si: (bi, gi, si, 0, 0)),
                  pl.BlockSpec((1, HEAD_DIM), lambda bi, gi, si: (0, 0))],
        out_specs=pl.BlockSpec((1, t, vw), lambda bi, gi, si: (bi, si, gi)),
        out_shape=jax.ShapeDtypeStruct((b, s, 2 * n_k_heads * HEAD_DIM), BF16),
        scratch_shapes=[pltpu.VMEM((2 * nkh, HEAD_DIM, HEAD_DIM), F32)],
        compiler_params=_params("parallel", "parallel", "arbitrary"),
        name="gdn_core",
    )(qkv_act, qkv_act, qkv_act, proj, beta, gc, gl, gct, norm_w)


def _router_kernel(logit_ref, idx_ref, w_ref, *, n_experts):
    x = logit_ref[...]
    lane = lax.broadcasted_iota(jnp.int32, x.shape, 1)
    x = jnp.where(lane < n_experts, x, -jnp.inf)
    m1 = jnp.max(x, axis=1, keepdims=True)
    i1 = jnp.min(jnp.where(x == m1, lane, LANES), axis=1, keepdims=True)
    x2 = jnp.where(lane == i1, -jnp.inf, x)
    m2 = jnp.max(x2, axis=1, keepdims=True)
    i2 = jnp.min(jnp.where(x2 == m2, lane, LANES), axis=1, keepdims=True)
    e2 = jnp.exp(m2 - m1)
    w1 = 1.0 / (1.0 + e2)
    w2 = e2 / (1.0 + e2)
    idx_ref[...] = jnp.where(lane == 0, i1, jnp.where(lane == 1, i2, 0))
    w_ref[...] = jnp.where(lane == 0, w1, jnp.where(lane == 1, w2, 0.0))


def router_top2(logits, n_experts, tm=1024):
    m, n = logits.shape
    spec = pl.BlockSpec((tm, n), lambda i: (i, 0))
    return pl.pallas_call(
        functools.partial(_router_kernel, n_experts=n_experts),
        grid=(m // tm,),
        in_specs=[spec],
        out_specs=[spec, spec],
        out_shape=[jax.ShapeDtypeStruct((m, n), jnp.int32), jax.ShapeDtypeStruct((m, n), F32)],
        compiler_params=_params("parallel"),
        name="router",
    )(logits)


MOE_ROW_TILE = 256
MOE_UP_COLS = 1408
MOE_DOWN_COLS = 1024


def _route(idx, n_experts, tm):
    m = idx.shape[0]
    a = 2 * m
    e = idx.reshape(a)
    onehot = (e[:, None] == jnp.arange(n_experts, dtype=jnp.int32)[None, :]).astype(jnp.int32)
    csum = jnp.cumsum(onehot, axis=0)
    counts = csum[-1]
    padded = (counts + tm - 1) // tm * tm
    ends = jnp.cumsum(padded)
    pos = jnp.sum(onehot * (csum - 1 + (ends - padded)[None, :]), axis=1)
    n_tiles = a // tm + n_experts
    tile_start = jnp.arange(n_tiles, dtype=jnp.int32) * tm
    tile_expert = jnp.minimum(jnp.sum((tile_start[:, None] >= ends[None, :]).astype(jnp.int32), axis=1),
                              n_experts - 1)
    n_used = (ends[-1] // tm).astype(jnp.int32).reshape(1)
    row_token = jnp.zeros((n_tiles * tm,), jnp.int32).at[pos].set(jnp.arange(a, dtype=jnp.int32) // 2)
    return pos.astype(jnp.int32), tile_expert, n_used, row_token


DMA_ISSUE_UNROLL = 8


def _row_copy(src_hbm, src_row, dst_vmem, dst_row, sem):
    return pltpu.make_async_copy(src_hbm.at[pl.ds(src_row, 1)], dst_vmem.at[pl.ds(dst_row, 1)], sem)


def _rows_wait(src_hbm, dst_vmem, sem):
    pltpu.make_async_copy(src_hbm.at[pl.ds(0, dst_vmem.shape[0])], dst_vmem, sem).wait()


def _moe_gather_kernel(row_token_ref, n_used_ref, h_hbm, g_ref, o_ref, buf, sem, *, tm):
    i = pl.program_id(0)
    n_used = n_used_ref[0]

    def issue(tile):
        slot = lax.rem(tile, 2)

        def start(r, c):
            _row_copy(h_hbm, row_token_ref[tile * tm + r], buf.at[slot], r, sem.at[slot]).start()
            return c

        lax.fori_loop(0, tm, start, 0, unroll=DMA_ISSUE_UNROLL)

    @pl.when(i == 0)
    def _():
        issue(i)

    @pl.when(i + 1 < n_used)
    def _():
        issue(i + 1)

    @pl.when(i < n_used)
    def _():
        slot = lax.rem(i, 2)
        _rows_wait(h_hbm, buf.at[slot], sem.at[slot])
        x = buf[slot]
        ms = jnp.mean(x * x, axis=-1, keepdims=True)
        o_ref[...] = (x * lax.rsqrt(ms + EPS) * g_ref[...]).astype(o_ref.dtype)

    @pl.when(i >= n_used)
    def _():
        o_ref[...] = jnp.zeros_like(o_ref)


def moe_gather(h, g, row_token, n_used, tm):
    m, d = h.shape
    p = row_token.shape[0]
    return pl.pallas_call(
        functools.partial(_moe_gather_kernel, tm=tm),
        grid_spec=pltpu.PrefetchScalarGridSpec(
            num_scalar_prefetch=2,
            grid=(p // tm,),
            in_specs=[pl.BlockSpec(memory_space=pl.ANY),
                      pl.BlockSpec((1, d), lambda i, rt, nu: (0, 0))],
            out_specs=pl.BlockSpec((tm, d), lambda i, rt, nu: (i, 0)),
            scratch_shapes=[pltpu.VMEM((2, tm, d), F32), pltpu.SemaphoreType.DMA((2,))]),
        out_shape=jax.ShapeDtypeStruct((p, d), BF16),
        compiler_params=_params("arbitrary"),
        name="moe_gather",
    )(row_token, n_used, h, g.reshape(1, d))


def _moe_matmul_kernel(te_ref, n_used_ref, x_ref, w_ref, *refs, gated):
    if gated:
        gate_ref, o_ref, wb_ref = refs
    else:
        o_ref, wb_ref = refs
    i = pl.program_id(1)
    n_used = n_used_ref[0]

    @pl.when((i == 0) | (te_ref[i] != te_ref[jnp.maximum(i - 1, 0)]))
    def _():
        wb_ref[...] = w_ref[...].astype(BF16)

    @pl.when(i < n_used)
    def _():
        r = _dot(x_ref[...], wb_ref[...])
        if gated:
            r = _silu(gate_ref[...].astype(F32)) * r
        o_ref[...] = r.astype(o_ref.dtype)

    @pl.when(i >= n_used)
    def _():
        o_ref[...] = jnp.zeros_like(o_ref)


def moe_matmul(xs, w, layer, col_block, n_cols, tn, out_dtype, tile_expert, n_used, tm, gate=None, name="moe_matmul"):
    p, kdim = xs.shape
    nj = n_cols // tn

    def row_tile(j, i, te, nu):
        return (jnp.minimum(i, nu[0] - 1), 0)

    in_specs = [pl.BlockSpec((tm, kdim), row_tile),
                pl.BlockSpec((None, None, kdim, tn), lambda j, i, te, nu: (layer, te[i], 0, j + col_block))]
    args = [tile_expert, n_used, xs, w]
    if gate is not None:
        in_specs.append(pl.BlockSpec((tm, tn), lambda j, i, te, nu: (jnp.minimum(i, nu[0] - 1), j)))
        args.append(gate)
    return pl.pallas_call(
        functools.partial(_moe_matmul_kernel, gated=gate is not None),
        grid_spec=pltpu.PrefetchScalarGridSpec(
            num_scalar_prefetch=2,
            grid=(nj, p // tm),
            in_specs=in_specs,
            out_specs=pl.BlockSpec((tm, tn), lambda j, i, te, nu: (i, j)),
            scratch_shapes=[pltpu.VMEM((kdim, tn), BF16)]),
        out_shape=jax.ShapeDtypeStruct((p, n_cols), out_dtype),
        compiler_params=_params("parallel", "arbitrary"),
        name=name,
    )(*args)


def _moe_combine_kernel(pos_ref, h_ref, w_ref, g_ref, y_hbm, *refs, tm, emit_h):
    if emit_h:
        o_ref, n_ref, buf, sem = refs
    else:
        n_ref, buf, sem = refs
    i = pl.program_id(0)

    def issue(tile):
        slot = lax.rem(tile, 2)

        def start(r, c):
            for k in range(2):
                _row_copy(y_hbm, pos_ref[2 * (tile * tm + r) + k], buf.at[slot, k], r, sem.at[slot]).start()
            return c

        lax.fori_loop(0, tm, start, 0, unroll=DMA_ISSUE_UNROLL)

    @pl.when(i == 0)
    def _():
        issue(i)

    @pl.when(i + 1 < pl.num_programs(0))
    def _():
        issue(i + 1)

    slot = lax.rem(i, 2)
    for k in range(2):
        _rows_wait(y_hbm, buf.at[slot, k], sem.at[slot])
    w = w_ref[...]
    x = h_ref[...] + w[:, 0:1] * buf[slot, 0] + w[:, 1:2] * buf[slot, 1]
    if emit_h:
        o_ref[...] = x
    ms = jnp.mean(x * x, axis=-1, keepdims=True)
    n_ref[...] = (x * lax.rsqrt(ms + EPS) * g_ref[...]).astype(n_ref.dtype)


def moe_combine(h, top_w, ys, pos, g_next, norm_dtype, emit_h, tm=256):
    m, d = h.shape
    row_block = pl.BlockSpec((tm, d), lambda i, pos: (i, 0))
    out_specs = [row_block, row_block] if emit_h else [row_block]
    out_shape = [jax.ShapeDtypeStruct((m, d), F32)] if emit_h else []
    out_shape.append(jax.ShapeDtypeStruct((m, d), norm_dtype))
    outs = pl.pallas_call(
        functools.partial(_moe_combine_kernel, tm=tm, emit_h=emit_h),
        grid_spec=pltpu.PrefetchScalarGridSpec(
            num_scalar_prefetch=1,
            grid=(m // tm,),
            in_specs=[row_block,
                      pl.BlockSpec((tm, LANES), lambda i, pos: (i, 0)),
                      pl.BlockSpec((1, d), lambda i, pos: (0, 0)),
                      pl.BlockSpec(memory_space=pl.ANY)],
            out_specs=out_specs,
            scratch_shapes=[pltpu.VMEM((2, 2, tm, d), F32), pltpu.SemaphoreType.DMA((2,))]),
        out_shape=out_shape,
        compiler_params=_params("arbitrary"),
        name="moe_combine",
    )(pos, h, top_w, g_next.reshape(1, d), ys)
    return (outs[0], outs[1]) if emit_h else (None, outs[0])


def _sb_layer(h, hn, w_in, w_out, layer):
    b, s, d = h.shape
    n_heads = w_out.shape[1] // HEAD_DIM
    qkv = matmul(hn, w_in, layer, w_in.shape[2], BF16)
    o = sb_attention(qkv.reshape(b, s, -1), n_heads)
    return matmul(o.reshape(b * s, -1), w_out, layer, d, F32, res=h.reshape(b * s, d)).reshape(b, s, d)


def _fox_layer(h, hn, g, w_in, b_f, w_out, layer):
    b, s, d = h.shape
    n_heads = w_out.shape[1] // HEAD_DIM
    main = 4 * n_heads * HEAD_DIM
    h2 = h.reshape(b * s, d)
    proj = matmul(hn, w_in, layer, main, BF16).reshape(b, s, main)
    f_raw = small_proj(h2, g, _pad_cols(w_in[layer, :, main:], LANES)).reshape(b, s, LANES)
    bias = jnp.pad(b_f, (0, LANES - n_heads)).reshape(1, LANES)
    cum_col, cum_t = fox_gate(f_raw, bias)
    o = fox_attention(proj, cum_col, cum_t[:, :n_heads, :], n_heads)
    return matmul(o.reshape(b * s, -1), w_out, layer, d, F32, res=h2).reshape(b, s, d)


def _gdn_layer(h, hn, g, w_in, conv_w, a_log, dt_bias, norm_w, w_out, layer):
    b, s, d = h.shape
    v_dim = w_out.shape[1]
    n_v_heads = v_dim // HEAD_DIM
    n_k_heads = n_v_heads // 2
    k_dim = n_k_heads * HEAD_DIM
    qkv_dim = 2 * k_dim + v_dim
    main = qkv_dim + v_dim
    h2 = h.reshape(b * s, d)
    proj = matmul(hn, w_in, layer, main, BF16).reshape(b, s, main)
    w_gates = jnp.concatenate([_pad_cols(w_in[layer, :, main:main + n_v_heads], LANES),
                               _pad_cols(w_in[layer, :, main + n_v_heads:], LANES)], axis=1)
    raw = small_proj(h2, g, w_gates).reshape(b, s, 2 * LANES)
    pad = (0, LANES - n_v_heads)
    beta, gc, gl, gct = gdn_gates(raw[..., :LANES], raw[..., LANES:],
                                  jnp.pad(a_log, pad).reshape(1, LANES), jnp.pad(dt_bias, pad).reshape(1, LANES))
    gct = gct[:, :n_v_heads, :].reshape(b, n_k_heads, 2, s // GDN_CHUNK, GDN_CHUNK).transpose(0, 1, 3, 2, 4)
    qkv_act = gdn_prep(proj, conv_w.T, qkv_dim, k_dim)
    o = gdn_core(qkv_act, proj, beta, gc, gl, gct, norm_w.reshape(1, HEAD_DIM), n_k_heads)
    return matmul(o.reshape(b * s, v_dim), w_out, layer, d, F32, res=h2).reshape(b, s, d)


def _dense_ffn(h, g, w_gate_up, w_down, layer):
    b, s, d = h.shape
    h2 = h.reshape(b * s, d)
    hn = rmsnorm(h2, g, BF16)
    mid = swiglu_up(hn, w_gate_up, layer)
    return matmul(mid, w_down, layer, d, F32, res=h2).reshape(b, s, d)


def _moe_ffn(h, g, w_router, w_gate_up, w_down, layer, g_next, norm_dtype, emit_h):
    b, s, d = h.shape
    n_experts = w_router.shape[1]
    f = w_down.shape[2]
    h2 = h.reshape(b * s, d)
    logits = small_proj(h2, g, _pad_cols(w_router, LANES))
    top_idx, top_w = router_top2(logits, n_experts)
    tm = MOE_ROW_TILE
    pos, tile_expert, n_used, row_token = _route(top_idx[:, :2], n_experts, tm)
    xs = moe_gather(h2, g, row_token, n_used, tm)
    tn_up = _tile(f, MOE_UP_COLS)
    gate = moe_matmul(xs, w_gate_up, layer, 0, f, tn_up, BF16, tile_expert, n_used, tm, name="moe_gate")
    mid = moe_matmul(xs, w_gate_up, layer, f // tn_up, f, tn_up, BF16, tile_expert, n_used, tm, gate=gate,
                     name="moe_up")
    ys = moe_matmul(mid, w_down, layer, 0, d, _tile(d, MOE_DOWN_COLS), F32, tile_expert, n_used, tm,
                    name="moe_down")
    new_h, normed = moe_combine(h2, top_w, ys, pos, g_next, norm_dtype, emit_h)
    return (new_h.reshape(b, s, d) if emit_h else None), normed


def kernel(x, norm_mix, norm_ffn, sb_w_in, sb_w_out, gdn_w_in, gdn_conv_w, gdn_a_log, gdn_dt_bias, gdn_norm_w,
           gdn_w_out, fox_w_in, fox_b_f, fox_w_out, ffn_w_gate_up, ffn_w_down, moe_w_router, moe_w_gate_up,
           moe_w_down, final_norm):
    depth = norm_mix.shape[0]
    b, s, d = x.shape
    h = x
    hn = None
    for i in range(depth):
        kind, j = i % 3, i // 3
        if hn is None:
            hn = rmsnorm(h.reshape(b * s, d), norm_mix[i], BF16)
        if kind == 0:
            h = _sb_layer(h, hn, sb_w_in, sb_w_out, j)
        elif kind == 1:
            h = _gdn_layer(h, hn, norm_mix[i], gdn_w_in, gdn_conv_w[j], gdn_a_log[j], gdn_dt_bias[j],
                           gdn_norm_w[j], gdn_w_out, j)
        else:
            h = _fox_layer(h, hn, norm_mix[i], fox_w_in, fox_b_f[j], fox_w_out, j)
        hn = None
        f = i // 2
        last = i == depth - 1
        if i % 2 == 0:
            h = _dense_ffn(h, norm_ffn[i], ffn_w_gate_up, ffn_w_down, f)
        elif last:
            _, out = _moe_ffn(h, norm_ffn[i], moe_w_router[f], moe_w_gate_up, moe_w_down, f,
                              final_norm, x.dtype, emit_h=False)
            return out.reshape(b, s, d)
        else:
            h, hn = _moe_ffn(h, norm_ffn[i], moe_w_router[f], moe_w_gate_up, moe_w_down, f,
                             norm_mix[i + 1], BF16, emit_h=True)
    return rmsnorm(h.reshape(b * s, d), final_norm, x.dtype).reshape(b, s, d)
```

```python
import functools

import jax
import jax.numpy as jnp
from jax import lax
from jax.experimental import pallas as pl
from jax.experimental.pallas import tpu as pltpu

HEAD_DIM = 128
LANES = 128
GDN_CONV = 4
GDN_CHUNK = 128
EPS = 1e-6
LOG2E = 1.4426950408889634
EXP_UNDERFLOW = -104.0
VMEM_LIMIT_BYTES = 56 * 1024 * 1024

F32 = jnp.float32
BF16 = jnp.bfloat16


def _params(*semantics):
    return pltpu.CompilerParams(dimension_semantics=semantics, vmem_limit_bytes=VMEM_LIMIT_BYTES)


def _tile(dim, preferred):
    assert dim % LANES == 0, dim
    t = min(preferred, dim) // LANES * LANES
    while dim % t:
        t -= LANES
    return t


def _dot(a, b):
    return jnp.dot(a, b, preferred_element_type=F32)


def _split3(x):
    hi = x.astype(BF16)
    r = x - hi.astype(F32)
    mid = r.astype(BF16)
    lo = (r - mid.astype(F32)).astype(BF16)
    return hi, mid, lo


def _split2(x):
    hi = x.astype(BF16)
    lo = (x - hi.astype(F32)).astype(BF16)
    return hi, lo


def _softplus(z):
    return jnp.maximum(z, 0.0) + jnp.log(1.0 + jnp.exp(-jnp.abs(z)))


def _sigmoid(z):
    return 1.0 / (1.0 + jnp.exp(-z))


def _silu(z):
    return z * _sigmoid(z)


def _block_id(idx, size):
    shift = size.bit_length() - 1
    assert 1 << shift == size
    return jnp.right_shift(idx, shift)


def _rmsnorm_kernel(x_ref, g_ref, o_ref):
    x = x_ref[...]
    ms = jnp.mean(x * x, axis=-1, keepdims=True)
    o_ref[...] = (x * lax.rsqrt(ms + EPS) * g_ref[...]).astype(o_ref.dtype)


def rmsnorm(x, g, out_dtype, tm=512):
    m, d = x.shape
    return pl.pallas_call(
        _rmsnorm_kernel,
        grid=(m // tm,),
        in_specs=[pl.BlockSpec((tm, d), lambda i: (i, 0)),
                  pl.BlockSpec((1, d), lambda i: (0, 0))],
        out_specs=pl.BlockSpec((tm, d), lambda i: (i, 0)),
        out_shape=jax.ShapeDtypeStruct((m, d), out_dtype),
        compiler_params=_params("parallel"),
        name="rmsnorm",
    )(x, g.reshape(1, d))


WS_VMEM_BUDGET = 44 * 1024 * 1024


def _ws_tiles(m, kdim, n, n_weight_blocks, out_bytes, has_res):
    for tm_pref, tn_pref in ((1024, 1024), (1024, 512), (512, 512), (512, 256), (256, 256), (256, 128)):
        tm, tn = _tile(m, tm_pref), _tile(n, tn_pref)
        weights = n_weight_blocks * kdim * tn * (2 * 4 + 2)
        acts = 2 * tm * kdim * 2
        outs = 2 * tm * tn * (out_bytes + (4 if has_res else 0))
        results = n_weight_blocks * tm * tn * 4
        if weights + acts + outs + results <= WS_VMEM_BUDGET:
            return tm, tn
    raise ValueError(f"no weight-stationary tiling fits VMEM for {(m, kdim, n)}")


def _matmul_kernel(*refs, has_res):
    if has_res:
        a_ref, w_ref, res_ref, o_ref, wb_ref = refs
    else:
        a_ref, w_ref, o_ref, wb_ref = refs

    @pl.when(pl.program_id(1) == 0)
    def _():
        wb_ref[...] = w_ref[...].astype(BF16)

    r = _dot(a_ref[...], wb_ref[...])
    if has_res:
        r = r + res_ref[...]
    o_ref[...] = r.astype(o_ref.dtype)


def matmul(a, w, layer, n_cols, out_dtype, res=None):
    m, kdim = a.shape
    tm, tn = _ws_tiles(m, kdim, n_cols, 1, jnp.dtype(out_dtype).itemsize, res is not None)
    in_specs = [pl.BlockSpec((tm, kdim), lambda j, i: (i, 0)),
                pl.BlockSpec((kdim, tn), lambda j, i: (layer, j))]
    args = [a, w.reshape(-1, w.shape[2])]
    if res is not None:
        in_specs.append(pl.BlockSpec((tm, tn), lambda j, i: (i, j)))
        args.append(res)
    return pl.pallas_call(
        functools.partial(_matmul_kernel, has_res=res is not None),
        grid=(n_cols // tn, m // tm),
        in_specs=in_specs,
        out_specs=pl.BlockSpec((tm, tn), lambda j, i: (i, j)),
        out_shape=jax.ShapeDtypeStruct((m, n_cols), out_dtype),
        scratch_shapes=[pltpu.VMEM((kdim, tn), BF16)],
        compiler_params=_params("parallel", "arbitrary"),
        name="matmul",
    )(*args)


def _swiglu_up_kernel(a_ref, wg_ref, wu_ref, o_ref, wgb_ref, wub_ref):
    @pl.when(pl.program_id(1) == 0)
    def _():
        wgb_ref[...] = wg_ref[...].astype(BF16)
        wub_ref[...] = wu_ref[...].astype(BF16)

    a = a_ref[...]
    o_ref[...] = (_silu(_dot(a, wgb_ref[...])) * _dot(a, wub_ref[...])).astype(o_ref.dtype)


def swiglu_up(a, w_gate_up, layer):
    m, kdim = a.shape
    f = w_gate_up.shape[2] // 2
    tm, tn = _ws_tiles(m, kdim, f, 2, 2, False)
    nj = f // tn
    return pl.pallas_call(
        _swiglu_up_kernel,
        grid=(nj, m // tm),
        in_specs=[pl.BlockSpec((tm, kdim), lambda j, i: (i, 0)),
                  pl.BlockSpec((None, kdim, tn), lambda j, i: (layer, 0, j)),
                  pl.BlockSpec((None, kdim, tn), lambda j, i: (layer, 0, j + nj))],
        out_specs=pl.BlockSpec((tm, tn), lambda j, i: (i, j)),
        out_shape=jax.ShapeDtypeStruct((m, f), BF16),
        scratch_shapes=[pltpu.VMEM((kdim, tn), BF16), pltpu.VMEM((kdim, tn), BF16)],
        compiler_params=_params("parallel", "arbitrary"),
        name="swiglu_up",
    )(a, w_gate_up, w_gate_up)


def _small_proj_kernel(h_ref, g_ref, w_ref, o_ref):
    x = h_ref[...]
    ms = jnp.mean(x * x, axis=-1, keepdims=True)
    xn = x * lax.rsqrt(ms + EPS) * g_ref[...]
    xh, xl = _split2(xn)
    wh, wl = _split2(w_ref[...])
    o_ref[...] = _dot(xh, wh) + _dot(xl, wh) + _dot(xh, wl)


def small_proj(h, g, w, tm=512):
    m, d = h.shape
    n = w.shape[1]
    return pl.pallas_call(
        _small_proj_kernel,
        grid=(m // tm,),
        in_specs=[pl.BlockSpec((tm, d), lambda i: (i, 0)),
                  pl.BlockSpec((1, d), lambda i: (0, 0)),
                  pl.BlockSpec((d, n), lambda i: (0, 0))],
        out_specs=pl.BlockSpec((tm, n), lambda i: (i, 0)),
        out_shape=jax.ShapeDtypeStruct((m, n), F32),
        compiler_params=_params("parallel"),
        name="small_proj",
    )(h, g.reshape(1, d), w)


def _pad_cols(w, width):
    return jnp.pad(w, ((0, 0), (0, width - w.shape[1])))


def _split_heads(x, n):
    return jnp.stack([x[:, h * HEAD_DIM:(h + 1) * HEAD_DIM] for h in range(n)], axis=0)


def _sb_attn_kernel(q_ref, k_ref, v_ref, o_ref, *, t, hb, scale):
    i = pl.program_id(2)
    q = _split_heads(q_ref[0], hb)
    row = lax.broadcasted_iota(jnp.int32, (t, t), 0)
    col = lax.broadcasted_iota(jnp.int32, (t, t), 1)
    strict = col < row
    suffix = jnp.where(row > col, 1.0, 0.0).astype(BF16)

    def block(kb, carry, acc, masked):
        start = pl.multiple_of(kb * t, t)
        k = _split_heads(k_ref[0, pl.ds(start, t), :], hb)
        v = _split_heads(v_ref[0, pl.ds(start, t), :], hb)
        z = _bmm_nt(q, k) * scale
        sp = _softplus(z)
        log_keep = -sp
        if masked:
            log_keep = jnp.where(strict, log_keep, 0.0)
        hi, lo = _split2(log_keep.reshape(hb * t, t))
        later = (_dot(hi, suffix) + _dot(lo, suffix)).reshape(hb, t, t) + carry
        w = jnp.exp(z - sp + later)
        if masked:
            w = jnp.where(strict, w, 0.0)
        acc = acc + _bmm(w.astype(BF16), v)
        carry = carry + jnp.sum(log_keep, axis=2, keepdims=True)
        return carry, acc

    carry, acc = block(i, jnp.zeros((hb, t, 1), F32), jnp.zeros((hb, t, HEAD_DIM), F32), True)

    def live(c):
        return (c[0] < i) & (c[3] > EXP_UNDERFLOW)

    def body(c):
        carry, acc = block(i - 1 - c[0], c[1], c[2], False)
        return c[0] + 1, carry, acc, jnp.max(carry)

    _, carry, acc, _ = lax.while_loop(live, body, (jnp.int32(0), carry, acc, jnp.max(carry)))
    for h in range(hb):
        o_ref[0, :, h * HEAD_DIM:(h + 1) * HEAD_DIM] = acc[h].astype(o_ref.dtype)


def sb_attention(qkv, n_heads, t=256, hb=4):
    b, s, _ = qkv.shape
    hb = min(hb, n_heads)
    assert n_heads % hb == 0
    g = n_heads // hb
    w = hb * HEAD_DIM
    return pl.pallas_call(
        functools.partial(_sb_attn_kernel, t=t, hb=hb, scale=HEAD_DIM ** -0.5),
        grid=(b, g, s // t),
        in_specs=[pl.BlockSpec((1, t, w), lambda bi, gi, i: (bi, i, gi)),
                  pl.BlockSpec((1, s, w), lambda bi, gi, i: (bi, 0, g + gi)),
                  pl.BlockSpec((1, s, w), lambda bi, gi, i: (bi, 0, 2 * g + gi))],
        out_specs=pl.BlockSpec((1, t, w), lambda bi, gi, i: (bi, i, gi)),
        out_shape=jax.ShapeDtypeStruct((b, s, n_heads * HEAD_DIM), BF16),
        compiler_params=_params("parallel", "parallel", "arbitrary"),
        name="sb_attention",
    )(qkv, qkv, qkv)


def _fox_gate_kernel(f_ref, bias_ref, col_ref, row_ref, carry_ref, *, t):
    i = pl.program_id(1)

    @pl.when(i == 0)
    def _():
        carry_ref[...] = jnp.zeros_like(carry_ref)

    x = f_ref[0] + bias_ref[...]
    log_f = -_softplus(-x)
    row = lax.broadcasted_iota(jnp.int32, (t, t), 0)
    col = lax.broadcasted_iota(jnp.int32, (t, t), 1)
    lower = jnp.where(col <= row, 1.0, 0.0).astype(BF16)
    hi, mid, lo = _split3(log_f)
    cum = _dot(lower, hi) + _dot(lower, mid) + _dot(lower, lo) + carry_ref[...]
    col_ref[0] = cum
    row_ref[0] = cum.T
    carry_ref[...] = cum[t - 1:t, :]


def fox_gate(f_raw, bias, t=512):
    b, s, n = f_raw.shape
    return pl.pallas_call(
        functools.partial(_fox_gate_kernel, t=t),
        grid=(b, s // t),
        in_specs=[pl.BlockSpec((1, t, n), lambda bi, i: (bi, i, 0)),
                  pl.BlockSpec((1, n), lambda bi, i: (0, 0))],
        out_specs=[pl.BlockSpec((1, t, n), lambda bi, i: (bi, i, 0)),
                   pl.BlockSpec((1, n, t), lambda bi, i: (bi, 0, i))],
        out_shape=[jax.ShapeDtypeStruct((b, s, n), F32), jax.ShapeDtypeStruct((b, n, s), F32)],
        scratch_shapes=[pltpu.VMEM((1, n), F32)],
        compiler_params=_params("parallel", "arbitrary"),
        name="fox_gate",
    )(f_raw, bias)


def _fox_attn_kernel(q_ref, k_ref, v_ref, gate_ref, cq_ref, ck_ref, o_ref, *, t, hb, scale):
    gi = pl.program_id(1)
    i = pl.program_id(2)
    q = _split_heads(q_ref[0], hb)
    lane = lax.broadcasted_iota(jnp.int32, (t, LANES), 1)
    cq = cq_ref[0]
    cum_q = jnp.stack([jnp.sum(jnp.where(lane == gi * hb + h, cq, 0.0), axis=1, keepdims=True)
                       for h in range(hb)], axis=0) * LOG2E
    row = lax.broadcasted_iota(jnp.int32, (t, t), 0)
    col = lax.broadcasted_iota(jnp.int32, (t, t), 1)
    causal = col <= row

    def block(kb, m, l, acc, masked):
        start = pl.multiple_of(kb * t, t)
        k = _split_heads(k_ref[0, pl.ds(start, t), :], hb)
        v = _split_heads(v_ref[0, pl.ds(start, t), :], hb)
        cum_k = ck_ref[0, :, pl.ds(kb, 1), :] * LOG2E
        logits = _bmm_nt(q, k) * (scale * LOG2E) + cum_q - cum_k
        if masked:
            logits = jnp.where(causal, logits, -jnp.inf)
        m_new = jnp.maximum(m, jnp.max(logits, axis=2, keepdims=True))
        alpha = jnp.exp2(m - m_new)
        p = jnp.exp2(logits - m_new)
        l = l * alpha + jnp.sum(p, axis=2, keepdims=True)
        acc = acc * alpha + _bmm(p.astype(BF16), v)
        return m_new, l, acc

    init = (jnp.full((hb, t, 1), -jnp.inf, F32), jnp.zeros((hb, t, 1), F32), jnp.zeros((hb, t, HEAD_DIM), F32))
    m, l, acc = block(i, *init, True)

    def body(n, c):
        return block(i - 1 - n, *c, False)

    m, l, acc = lax.fori_loop(0, i, body, (m, l, acc))
    out = acc / l
    for h in range(hb):
        hs = slice(h * HEAD_DIM, (h + 1) * HEAD_DIM)
        o_ref[0, :, hs] = (out[h] * _sigmoid(gate_ref[0, :, hs].astype(F32))).astype(o_ref.dtype)


def fox_attention(proj, cum_col, cum_row, n_heads, t=512, hb=4):
    b, s, _ = proj.shape
    hb = min(hb, n_heads)
    assert n_heads % hb == 0
    g = n_heads // hb
    w = hb * HEAD_DIM
    cum_row = cum_row.reshape(b, n_heads, s // t, t)
    return pl.pallas_call(
        functools.partial(_fox_attn_kernel, t=t, hb=hb, scale=HEAD_DIM ** -0.5),
        grid=(b, g, s // t),
        in_specs=[pl.BlockSpec((1, t, w), lambda bi, gi, i: (bi, i, gi)),
                  pl.BlockSpec((1, s, w), lambda bi, gi, i: (bi, 0, g + gi)),
                  pl.BlockSpec((1, s, w), lambda bi, gi, i: (bi, 0, 2 * g + gi)),
                  pl.BlockSpec((1, t, w), lambda bi, gi, i: (bi, i, 3 * g + gi)),
                  pl.BlockSpec((1, t, LANES), lambda bi, gi, i: (bi, i, 0)),
                  pl.BlockSpec((1, hb, s // t, t), lambda bi, gi, i: (bi, gi, 0, 0))],
        out_specs=pl.BlockSpec((1, t, w), lambda bi, gi, i: (bi, i, gi)),
        out_shape=jax.ShapeDtypeStruct((b, s, n_heads * HEAD_DIM), BF16),
        compiler_params=_params("parallel", "parallel", "arbitrary"),
        name="fox_attention",
    )(proj, proj, proj, proj, cum_col, cum_row)


def _gdn_gate_kernel(b_ref, a_ref, alog_ref, dt_ref, beta_ref, gc_ref, gl_ref, gct_ref, *, t, chunk):
    beta_ref[0] = _sigmoid(b_ref[0])
    g = -jnp.exp(alog_ref[...]) * _softplus(a_ref[0] + dt_ref[...])
    row = lax.broadcasted_iota(jnp.int32, (t, t), 0)
    col = lax.broadcasted_iota(jnp.int32, (t, t), 1)
    same = _block_id(row, chunk) == _block_id(col, chunk)
    lower = jnp.where(same & (col <= row), 1.0, 0.0).astype(BF16)
    whole = jnp.where(same, 1.0, 0.0).astype(BF16)
    hi, mid, lo = _split3(g)
    gc = _dot(lower, hi) + _dot(lower, mid) + _dot(lower, lo)
    gc_ref[0] = gc
    gl_ref[0] = _dot(whole, hi) + _dot(whole, mid) + _dot(whole, lo)
    gct_ref[0] = gc.T


def gdn_gates(b_raw, a_raw, a_log, dt_bias, t=512):
    b, s, n = b_raw.shape
    spec = pl.BlockSpec((1, t, n), lambda bi, i: (bi, i, 0))
    vec = pl.BlockSpec((1, n), lambda bi, i: (0, 0))
    return pl.pallas_call(
        functools.partial(_gdn_gate_kernel, t=t, chunk=GDN_CHUNK),
        grid=(b, s // t),
        in_specs=[spec, spec, vec, vec],
        out_specs=[spec, spec, spec, pl.BlockSpec((1, n, t), lambda bi, i: (bi, 0, i))],
        out_shape=[jax.ShapeDtypeStruct((b, s, n), F32)] * 3 + [jax.ShapeDtypeStruct((b, n, s), F32)],
        compiler_params=_params("parallel", "parallel"),
        name="gdn_gates",
    )(b_raw, a_raw, a_log, dt_bias)


def _gdn_prep_kernel(x_ref, prev_ref, w_ref, o_ref, *, ts, n_q_blocks, n_qk_blocks, heads_per_block):
    si = pl.program_id(1)
    ci = pl.program_id(2)
    cur = x_ref[0].astype(F32)
    prev = jnp.where(si == 0, 0.0, prev_ref[0].astype(F32))
    ext = jnp.concatenate([prev, cur], axis=0)
    w = w_ref[...]
    y = jnp.zeros_like(cur)
    for tap in range(GDN_CONV):
        off = 8 - (GDN_CONV - 1) + tap
        y = y + w[tap:tap + 1, :] * ext[off:off + ts, :]
    y = _silu(y)
    q_scale = jnp.where(ci < n_q_blocks, HEAD_DIM ** -0.5, 1.0)
    is_qk = ci < n_qk_blocks
    for hh in range(heads_per_block):
        sl = slice(hh * HEAD_DIM, (hh + 1) * HEAD_DIM)
        yh = y[:, sl]
        ss = jnp.sum(yh * yh, axis=1, keepdims=True)
        factor = jnp.where(is_qk, lax.rsqrt(ss + EPS) * q_scale, 1.0)
        o_ref[0, :, sl] = (yh * factor).astype(o_ref.dtype)


def gdn_prep(proj, conv_w_t, qkv_dim, k_dim, ts=512, tc=512):
    b, s, _ = proj.shape
    ts, tc = min(ts, s), min(tc, k_dim)
    rows8 = ts // 8
    return pl.pallas_call(
        functools.partial(_gdn_prep_kernel, ts=ts, n_q_blocks=k_dim // tc, n_qk_blocks=2 * k_dim // tc,
                          heads_per_block=tc // HEAD_DIM),
        grid=(b, s // ts, qkv_dim // tc),
        in_specs=[pl.BlockSpec((1, ts, tc), lambda bi, si, ci: (bi, si, ci)),
                  pl.BlockSpec((1, 8, tc), lambda bi, si, ci: (bi, jnp.maximum(si * rows8 - 1, 0), ci)),
                  pl.BlockSpec((GDN_CONV, tc), lambda bi, si, ci: (0, ci))],
        out_specs=pl.BlockSpec((1, ts, tc), lambda bi, si, ci: (bi, si, ci)),
        out_shape=jax.ShapeDtypeStruct((b, s, qkv_dim), BF16),
        compiler_params=_params("parallel", "parallel", "parallel"),
        name="gdn_prep",
    )(proj, proj, conv_w_t)


def _bmm(a, b):
    return lax.dot_general(a, b, (((2,), (1,)), ((0,), (0,))), preferred_element_type=F32)


def _bmm_nt(a, b):
    return lax.dot_general(a, b, (((2,), (2,)), ((0,), (0,))), preferred_element_type=F32)


def _bmm_tn(a, b):
    return lax.dot_general(a, b, (((1,), (1,)), ((0,), (0,))), preferred_element_type=F32)


def _inv_unit_lower(m_strict, row, col):
    c = m_strict.shape[-1]
    eye = jnp.where(row == col, 1.0, 0.0)
    n1 = jnp.where(_block_id(row, 8) == _block_id(col, 8), -m_strict, 0.0)
    n1b = n1.astype(BF16)
    n2 = _bmm(n1b, n1b)
    n2b = n2.astype(BF16)
    n4 = _bmm(n2b, n2b)
    x = _bmm((eye + n1).astype(BF16), (eye + n2).astype(BF16))
    x = _bmm(x.astype(BF16), (eye + n4).astype(BF16))
    size = 8
    while size < c:
        lower_left = ((_block_id(row, size) == _block_id(col, size) + 1)
                      & (_block_id(row, 2 * size) == _block_id(col, 2 * size)))
        cm = jnp.where(lower_left, m_strict, 0.0).astype(BF16)
        xb = x.astype(BF16)
        x = x - _bmm(_bmm(xb, cm).astype(BF16), xb)
        size *= 2
    return x


def _gdn_kernel(q_ref, k_ref, v_ref, z_ref, beta_ref, gc_ref, gl_ref, gct_ref, nw_ref, o_ref, state_ref,
                *, n_chunks, nkh):
    gi = pl.program_id(1)
    si = pl.program_id(2)
    c = GDN_CHUNK
    t = n_chunks * c
    nvh = 2 * nkh

    @pl.when(si == 0)
    def _():
        state_ref[...] = jnp.zeros_like(state_ref)

    lane = lax.broadcasted_iota(jnp.int32, (t, LANES), 1)
    row = lax.broadcasted_iota(jnp.int32, (c, c), 0)
    col = lax.broadcasted_iota(jnp.int32, (c, c), 1)
    strict = col < row
    incl = col <= row

    nc = n_chunks

    def column(ref, head):
        col_vec = jnp.sum(jnp.where(lane == head, ref[0], 0.0), axis=1, keepdims=True)
        return col_vec.reshape(nc, c, 1)

    def heads(per_head):
        return jnp.concatenate([per_head(vh) for vh in range(nvh)], axis=0)

    def head_cols(ref, vh):
        return ref[0, :, vh * HEAD_DIM:(vh + 1) * HEAD_DIM].reshape(nc, c, HEAD_DIM)

    beta = heads(lambda vh: column(beta_ref, gi * nvh + vh))
    gcum = heads(lambda vh: column(gc_ref, gi * nvh + vh))
    glast = heads(lambda vh: column(gl_ref, gi * nvh + vh))
    g_row = heads(lambda vh: gct_ref[0, vh // 2, :, vh % 2:vh % 2 + 1, :])
    q = [head_cols(q_ref, kh) for kh in range(nkh)]
    k = [head_cols(k_ref, kh) for kh in range(nkh)]
    kk = [_bmm_nt(k[kh], k[kh]) for kh in range(nkh)]
    qk = [_bmm_nt(q[kh], k[kh]) for kh in range(nkh)]
    qf = heads(lambda vh: q[vh // 2].astype(F32))
    kf = heads(lambda vh: k[vh // 2].astype(F32))
    v = heads(lambda vh: head_cols(v_ref, vh)).astype(F32)

    decay = jnp.exp(jnp.minimum(gcum - g_row, 0.0))
    m_strict = jnp.where(strict, heads(lambda vh: kk[vh // 2]) * beta * decay, 0.0)
    attn = jnp.where(incl, heads(lambda vh: qk[vh // 2]) * decay, 0.0).astype(BF16)
    t_inv = _inv_unit_lower(m_strict, row, col).astype(BF16)
    u = _bmm(t_inv, (v * beta).astype(BF16))
    w = _bmm(t_inv, (kf * (beta * jnp.exp(gcum))).astype(BF16)).astype(BF16)
    q_dec = (qf * jnp.exp(gcum)).astype(BF16)
    k_dec = (kf * jnp.exp(glast - gcum)).astype(BF16)
    chunk_decay = jnp.exp(glast[:, 0:1, :])
    nw = nw_ref[...]

    state = state_ref[...]
    for ch in range(nc):
        def pick(x):
            return jnp.concatenate([x[vh * nc + ch:vh * nc + ch + 1] for vh in range(nvh)], axis=0)

        state_b = state.astype(BF16)
        v_new = pick(u) - _bmm(pick(w), state_b)
        v_new_b = v_new.astype(BF16)
        o = _bmm(pick(q_dec), state_b) + _bmm(pick(attn), v_new_b)
        state = state * pick(chunk_decay) + _bmm_tn(pick(k_dec), v_new_b)
        ms = jnp.mean(o * o, axis=2, keepdims=True)
        rows = slice(ch * c, (ch + 1) * c)
        zf = z_ref[0, rows, :].astype(F32)
        for vh in range(nvh):
            hs = slice(vh * HEAD_DIM, (vh + 1) * HEAD_DIM)
            o_ref[0, rows, hs] = (o[vh] * lax.rsqrt(ms[vh] + EPS) * nw * _silu(zf[:, hs])).astype(o_ref.dtype)
    state_ref[...] = state


def gdn_core(qkv_act, proj, beta, gc, gl, gct, norm_w, n_k_heads, n_chunks=4, nkh=2):
    b, s, _ = qkv_act.shape
    nkh = min(nkh, n_k_heads)
    assert n_k_heads % nkh == 0
    g = n_k_heads // nkh
    kw, vw = nkh * HEAD_DIM, 2 * nkh * HEAD_DIM
    n_chunks = min(n_chunks, s // GDN_CHUNK)
    t = n_chunks * GDN_CHUNK
    gate = pl.BlockSpec((1, t, LANES), lambda bi, gi, si: (bi, si, 0))
    return pl.pallas_call(
        functools.partial(_gdn_kernel, n_chunks=n_chunks, nkh=nkh),
        grid=(b, g, s // t),
        in_specs=[pl.BlockSpec((1, t, kw), lambda bi, gi, si: (bi, si, gi)),
                  pl.BlockSpec((1, t, kw), lambda bi, gi, si: (bi, si, g + gi)),
                  pl.BlockSpec((1, t, vw), lambda bi, gi, si: (bi, si, g + gi)),
                  pl.BlockSpec((1, t, vw), lambda bi, gi, si: (bi, si, 2 * g + gi)),
                  gate, gate, gate,
                  pl.BlockSpec((1, nkh, n_chunks, 2, GDN_CHUNK), lambda bi, gi, si: (bi, gi, si, 0, 0)),
                  pl.BlockSpec((1, HEAD_DIM), lambda bi, gi, si: (0, 0))],
        out_specs=pl.BlockSpec((1, t, vw), lambda bi, gi, si: (bi, si, gi)),
        out_shape=jax.ShapeDtypeStruct((b, s, 2 * n_k_heads * HEAD_DIM), BF16),
        scratch_shapes=[pltpu.VMEM((2 * nkh, HEAD_DIM, HEAD_DIM), F32)],
        compiler_params=_params("parallel", "parallel", "arbitrary"),
        name="gdn_core",
    )(qkv_act, qkv_act, qkv_act, proj, beta, gc, gl, gct, norm_w)


def _router_kernel(logit_ref, idx_ref, w_ref, *, n_experts):
    x = logit_ref[...]
    lane = lax.broadcasted_iota(jnp.int32, x.shape, 1)
    x = jnp.where(lane < n_experts, x, -jnp.inf)
    m1 = jnp.max(x, axis=1, keepdims=True)
    i1 = jnp.min(jnp.where(x == m1, lane, LANES), axis=1, keepdims=True)
    x2 = jnp.where(lane == i1, -jnp.inf, x)
    m2 = jnp.max(x2, axis=1, keepdims=True)
    i2 = jnp.min(jnp.where(x2 == m2, lane, LANES), axis=1, keepdims=True)
    e2 = jnp.exp(m2 - m1)
    w1 = 1.0 / (1.0 + e2)
    w2 = e2 / (1.0 + e2)
    idx_ref[...] = jnp.where(lane == 0, i1, jnp.where(lane == 1, i2, 0))
    w_ref[...] = jnp.where(lane == 0, w1, jnp.where(lane == 1, w2, 0.0))


def router_top2(logits, n_experts, tm=1024):
    m, n = logits.shape
    spec = pl.BlockSpec((tm, n), lambda i: (i, 0))
    return pl.pallas_call(
        functools.partial(_router_kernel, n_experts=n_experts),
        grid=(m // tm,),
        in_specs=[spec],
        out_specs=[spec, spec],
        out_shape=[jax.ShapeDtypeStruct((m, n), jnp.int32), jax.ShapeDtypeStruct((m, n), F32)],
        compiler_params=_params("parallel"),
        name="router",
    )(logits)


MOE_ROW_TILE = 512
MOE_UP_COLS = 1408
MOE_DOWN_COLS = 1024


def _route(idx, n_experts, tm):
    m = idx.shape[0]
    a = 2 * m
    e = idx.reshape(a)
    onehot = (e[:, None] == jnp.arange(n_experts, dtype=jnp.int32)[None, :]).astype(jnp.int32)
    csum = jnp.cumsum(onehot, axis=0)
    counts = csum[-1]
    padded = (counts + tm - 1) // tm * tm
    ends = jnp.cumsum(padded)
    pos = jnp.sum(onehot * (csum - 1 + (ends - padded)[None, :]), axis=1)
    n_tiles = a // tm + n_experts
    tile_start = jnp.arange(n_tiles, dtype=jnp.int32) * tm
    tile_expert = jnp.minimum(jnp.sum((tile_start[:, None] >= ends[None, :]).astype(jnp.int32), axis=1),
                              n_experts - 1)
    n_used = (ends[-1] // tm).astype(jnp.int32).reshape(1)
    row_token = jnp.zeros((n_tiles * tm,), jnp.int32).at[pos].set(jnp.arange(a, dtype=jnp.int32) // 2)
    return pos.astype(jnp.int32), tile_expert, n_used, row_token


DMA_ISSUE_UNROLL = 8


def _row_copy(src_hbm, src_row, dst_vmem, dst_row, sem):
    return pltpu.make_async_copy(src_hbm.at[pl.ds(src_row, 1)], dst_vmem.at[pl.ds(dst_row, 1)], sem)


def _rows_wait(src_hbm, dst_vmem, sem):
    pltpu.make_async_copy(src_hbm.at[pl.ds(0, dst_vmem.shape[0])], dst_vmem, sem).wait()


def _moe_gather_kernel(row_token_ref, n_used_ref, h_hbm, g_ref, o_ref, buf, sem, *, tm):
    i = pl.program_id(0)
    n_used = n_used_ref[0]

    def issue(tile):
        slot = lax.rem(tile, 2)

        def start(r, c):
            _row_copy(h_hbm, row_token_ref[tile * tm + r], buf.at[slot], r, sem.at[slot]).start()
            return c

        lax.fori_loop(0, tm, start, 0, unroll=DMA_ISSUE_UNROLL)

    @pl.when(i == 0)
    def _():
        issue(i)

    @pl.when(i + 1 < n_used)
    def _():
        issue(i + 1)

    @pl.when(i < n_used)
    def _():
        slot = lax.rem(i, 2)
        _rows_wait(h_hbm, buf.at[slot], sem.at[slot])
        x = buf[slot]
        ms = jnp.mean(x * x, axis=-1, keepdims=True)
        o_ref[...] = (x * lax.rsqrt(ms + EPS) * g_ref[...]).astype(o_ref.dtype)

    @pl.when(i >= n_used)
    def _():
        o_ref[...] = jnp.zeros_like(o_ref)


def moe_gather(h, g, row_token, n_used, tm):
    m, d = h.shape
    p = row_token.shape[0]
    return pl.pallas_call(
        functools.partial(_moe_gather_kernel, tm=tm),
        grid_spec=pltpu.PrefetchScalarGridSpec(
            num_scalar_prefetch=2,
            grid=(p // tm,),
            in_specs=[pl.BlockSpec(memory_space=pl.ANY),
                      pl.BlockSpec((1, d), lambda i, rt, nu: (0, 0))],
            out_specs=pl.BlockSpec((tm, d), lambda i, rt, nu: (i, 0)),
            scratch_shapes=[pltpu.VMEM((2, tm, d), F32), pltpu.SemaphoreType.DMA((2,))]),
        out_shape=jax.ShapeDtypeStruct((p, d), BF16),
        compiler_params=_params("arbitrary"),
        name="moe_gather",
    )(row_token, n_used, h, g.reshape(1, d))


def _moe_matmul_kernel(te_ref, n_used_ref, x_ref, w_ref, *refs, gated):
    if gated:
        gate_ref, o_ref, wb_ref = refs
    else:
        o_ref, wb_ref = refs
    i = pl.program_id(1)
    n_used = n_used_ref[0]

    @pl.when((i == 0) | (te_ref[i] != te_ref[jnp.maximum(i - 1, 0)]))
    def _():
        wb_ref[...] = w_ref[...].astype(BF16)

    @pl.when(i < n_used)
    def _():
        r = _dot(x_ref[...], wb_ref[...])
        if gated:
            r = _silu(gate_ref[...].astype(F32)) * r
        o_ref[...] = r.astype(o_ref.dtype)

    @pl.when(i >= n_used)
    def _():
        o_ref[...] = jnp.zeros_like(o_ref)


def moe_matmul(xs, w, layer, col_block, n_cols, tn, out_dtype, tile_expert, n_used, tm, gate=None, name="moe_matmul"):
    p, kdim = xs.shape
    nj = n_cols // tn

    def row_tile(j, i, te, nu):
        return (jnp.minimum(i, nu[0] - 1), 0)

    in_specs = [pl.BlockSpec((tm, kdim), row_tile),
                pl.BlockSpec((None, None, kdim, tn), lambda j, i, te, nu: (layer, te[i], 0, j + col_block))]
    args = [tile_expert, n_used, xs, w]
    if gate is not None:
        in_specs.append(pl.BlockSpec((tm, tn), lambda j, i, te, nu: (jnp.minimum(i, nu[0] - 1), j)))
        args.append(gate)
    return pl.pallas_call(
        functools.partial(_moe_matmul_kernel, gated=gate is not None),
        grid_spec=pltpu.PrefetchScalarGridSpec(
            num_scalar_prefetch=2,
            grid=(nj, p // tm),
            in_specs=in_specs,
            out_specs=pl.BlockSpec((tm, tn), lambda j, i, te, nu: (i, j)),
            scratch_shapes=[pltpu.VMEM((kdim, tn), BF16)]),
        out_shape=jax.ShapeDtypeStruct((p, n_cols), out_dtype),
        compiler_params=_params("parallel", "arbitrary"),
        name=name,
    )(*args)


def _moe_combine_kernel(pos_ref, h_ref, w_ref, g_ref, y_hbm, *refs, tm, emit_h):
    if emit_h:
        o_ref, n_ref, buf, sem = refs
    else:
        n_ref, buf, sem = refs
    i = pl.program_id(0)

    def issue(tile):
        slot = lax.rem(tile, 2)

        def start(r, c):
            for k in range(2):
                _row_copy(y_hbm, pos_ref[2 * (tile * tm + r) + k], buf.at[slot, k], r, sem.at[slot]).start()
            return c

        lax.fori_loop(0, tm, start, 0, unroll=DMA_ISSUE_UNROLL)

    @pl.when(i == 0)
    def _():
        issue(i)

    @pl.when(i + 1 < pl.num_programs(0))
    def _():
        issue(i + 1)

    slot = lax.rem(i, 2)
    for k in range(2):
        _rows_wait(y_hbm, buf.at[slot, k], sem.at[slot])
    w = w_ref[...]
    x = h_ref[...] + w[:, 0:1] * buf[slot, 0] + w[:, 1:2] * buf[slot, 1]
    if emit_h:
        o_ref[...] = x
    ms = jnp.mean(x * x, axis=-1, keepdims=True)
    n_ref[...] = (x * lax.rsqrt(ms + EPS) * g_ref[...]).astype(n_ref.dtype)


def moe_combine(h, top_w, ys, pos, g_next, norm_dtype, emit_h, tm=256):
    m, d = h.shape
    row_block = pl.BlockSpec((tm, d), lambda i, pos: (i, 0))
    out_specs = [row_block, row_block] if emit_h else [row_block]
    out_shape = [jax.ShapeDtypeStruct((m, d), F32)] if emit_h else []
    out_shape.append(jax.ShapeDtypeStruct((m, d), norm_dtype))
    outs = pl.pallas_call(
        functools.partial(_moe_combine_kernel, tm=tm, emit_h=emit_h),
        grid_spec=pltpu.PrefetchScalarGridSpec(
            num_scalar_prefetch=1,
            grid=(m // tm,),
            in_specs=[row_block,
                      pl.BlockSpec((tm, LANES), lambda i, pos: (i, 0)),
                      pl.BlockSpec((1, d), lambda i, pos: (0, 0)),
                      pl.BlockSpec(memory_space=pl.ANY)],
            out_specs=out_specs,
            scratch_shapes=[pltpu.VMEM((2, 2, tm, d), F32), pltpu.SemaphoreType.DMA((2,))]),
        out_shape=out_shape,
        compiler_params=_params("arbitrary"),
        name="moe_combine",
    )(pos, h, top_w, g_next.reshape(1, d), ys)
    return (outs[0], outs[1]) if emit_h else (None, outs[0])


def _sb_layer(h, hn, w_in, w_out, layer):
    b, s, d = h.shape
    n_heads = w_out.shape[1] // HEAD_DIM
    qkv = matmul(hn, w_in, layer, w_in.shape[2], BF16)
    o = sb_attention(qkv.reshape(b, s, -1), n_heads)
    return matmul(o.reshape(b * s, -1), w_out, layer, d, F32, res=h.reshape(b * s, d)).reshape(b, s, d)


def _fox_layer(h, hn, g, w_in, b_f, w_out, layer):
    b, s, d = h.shape
    n_heads = w_out.shape[1] // HEAD_DIM
    main = 4 * n_heads * HEAD_DIM
    h2 = h.reshape(b * s, d)
    proj = matmul(hn, w_in, layer, main, BF16).reshape(b, s, main)
    f_raw = small_proj(h2, g, _pad_cols(w_in[layer, :, main:], LANES)).reshape(b, s, LANES)
    bias = jnp.pad(b_f, (0, LANES - n_heads)).reshape(1, LANES)
    cum_col, cum_t = fox_gate(f_raw, bias)
    o = fox_attention(proj, cum_col, cum_t[:, :n_heads, :], n_heads)
    return matmul(o.reshape(b * s, -1), w_out, layer, d, F32, res=h2).reshape(b, s, d)


def _gdn_layer(h, hn, g, w_in, conv_w, a_log, dt_bias, norm_w, w_out, layer):
    b, s, d = h.shape
    v_dim = w_out.shape[1]
    n_v_heads = v_dim // HEAD_DIM
    n_k_heads = n_v_heads // 2
    k_dim = n_k_heads * HEAD_DIM
    qkv_dim = 2 * k_dim + v_dim
    main = qkv_dim + v_dim
    h2 = h.reshape(b * s, d)
    proj = matmul(hn, w_in, layer, main, BF16).reshape(b, s, main)
    w_gates = jnp.concatenate([_pad_cols(w_in[layer, :, main:main + n_v_heads], LANES),
                               _pad_cols(w_in[layer, :, main + n_v_heads:], LANES)], axis=1)
    raw = small_proj(h2, g, w_gates).reshape(b, s, 2 * LANES)
    pad = (0, LANES - n_v_heads)
    beta, gc, gl, gct = gdn_gates(raw[..., :LANES], raw[..., LANES:],
                                  jnp.pad(a_log, pad).reshape(1, LANES), jnp.pad(dt_bias, pad).reshape(1, LANES))
    gct = gct[:, :n_v_heads, :].reshape(b, n_k_heads, 2, s // GDN_CHUNK, GDN_CHUNK).transpose(0, 1, 3, 2, 4)
    qkv_act = gdn_prep(proj, conv_w.T, qkv_dim, k_dim)
    o = gdn_core(qkv_act, proj, beta, gc, gl, gct, norm_w.reshape(1, HEAD_DIM), n_k_heads)
    return matmul(o.reshape(b * s, v_dim), w_out, layer, d, F32, res=h2).reshape(b, s, d)


def _dense_ffn(h, g, w_gate_up, w_down, layer):
    b, s, d = h.shape
    h2 = h.reshape(b * s, d)
    hn = rmsnorm(h2, g, BF16)
    mid = swiglu_up(hn, w_gate_up, layer)
    return matmul(mid, w_down, layer, d, F32, res=h2).reshape(b, s, d)


def _moe_ffn(h, g, w_router, w_gate_up, w_down, layer, g_next, norm_dtype, emit_h):
    b, s, d = h.shape
    n_experts = w_router.shape[1]
    f = w_down.shape[2]
    h2 = h.reshape(b * s, d)
    logits = small_proj(h2, g, _pad_cols(w_router, LANES))
    top_idx, top_w = router_top2(logits, n_experts)
    tm = MOE_ROW_TILE
    pos, tile_expert, n_used, row_token = _route(top_idx[:, :2], n_experts, tm)
    xs = moe_gather(h2, g, row_token, n_used, tm)
    tn_up = _tile(f, MOE_UP_COLS)
    gate = moe_matmul(xs, w_gate_up, layer, 0, f, tn_up, BF16, tile_expert, n_used, tm, name="moe_gate")
    mid = moe_matmul(xs, w_gate_up, layer, f // tn_up, f, tn_up, BF16, tile_expert, n_used, tm, gate=gate,
                     name="moe_up")
    ys = moe_matmul(mid, w_down, layer, 0, d, _tile(d, MOE_DOWN_COLS), F32, tile_expert, n_used, tm,
                    name="moe_down")
    new_h, normed = moe_combine(h2, top_w, ys, pos, g_next, norm_dtype, emit_h)
    return (new_h.reshape(b, s, d) if emit_h else None), normed


def kernel(x, norm_mix, norm_ffn, sb_w_in, sb_w_out, gdn_w_in, gdn_conv_w, gdn_a_log, gdn_dt_bias, gdn_norm_w,
           gdn_w_out, fox_w_in, fox_b_f, fox_w_out, ffn_w_gate_up, ffn_w_down, moe_w_router, moe_w_gate_up,
           moe_w_down, final_norm):
    depth = norm_mix.shape[0]
    b, s, d = x.shape
    h = x
    hn = None
    for i in range(depth):
        kind, j = i % 3, i // 3
        if hn is None:
            hn = rmsnorm(h.reshape(b * s, d), norm_mix[i], BF16)
        if kind == 0:
            h = _sb_layer(h, hn, sb_w_in, sb_w_out, j)
        elif kind == 1:
            h = _gdn_layer(h, hn, norm_mix[i], gdn_w_in, gdn_conv_w[j], gdn_a_log[j], gdn_dt_bias[j],
                           gdn_norm_w[j], gdn_w_out, j)
        else:
            h = _fox_layer(h, hn, norm_mix[i], fox_w_in, fox_b_f[j], fox_w_out, j)
        hn = None
        f = i // 2
        last = i == depth - 1
        if i % 2 == 0:
            h = _dense_ffn(h, norm_ffn[i], ffn_w_gate_up, ffn_w_down, f)
        elif last:
            _, out = _moe_ffn(h, norm_ffn[i], moe_w_router[f], moe_w_gate_up, moe_w_down, f,
                              final_norm, x.dtype, emit_h=False)
            return out.reshape(b, s, d)
        else:
            h, hn = _moe_ffn(h, norm_ffn[i], moe_w_router[f], moe_w_gate_up, moe_w_down, f,
                             norm_mix[i + 1], BF16, emit_h=True)
    return rmsnorm(h.reshape(b * s, d), final_norm, x.dtype).reshape(b, s, d)
```

```python
import functools

import jax
import jax.numpy as jnp
from jax import lax
from jax.experimental import pallas as pl
from jax.experimental.pallas import tpu as pltpu

HEAD_DIM = 128
LANES = 128
GDN_CONV = 4
GDN_CHUNK = 128
EPS = 1e-6
LOG2E = 1.4426950408889634
EXP_UNDERFLOW = -104.0
VMEM_LIMIT_BYTES = 56 * 1024 * 1024

F32 = jnp.float32
BF16 = jnp.bfloat16


def _params(*semantics):
    return pltpu.CompilerParams(dimension_semantics=semantics, vmem_limit_bytes=VMEM_LIMIT_BYTES)


def _tile(dim, preferred):
    assert dim % LANES == 0, dim
    t = min(preferred, dim) // LANES * LANES
    while dim % t:
        t -= LANES
    return t


def _dot(a, b):
    return jnp.dot(a, b, preferred_element_type=F32)


def _split3(x):
    hi = x.astype(BF16)
    r = x - hi.astype(F32)
    mid = r.astype(BF16)
    lo = (r - mid.astype(F32)).astype(BF16)
    return hi, mid, lo


def _split2(x):
    hi = x.astype(BF16)
    lo = (x - hi.astype(F32)).astype(BF16)
    return hi, lo


def _softplus(z):
    return jnp.maximum(z, 0.0) + jnp.log(1.0 + jnp.exp(-jnp.abs(z)))


def _sigmoid(z):
    return 1.0 / (1.0 + jnp.exp(-z))


def _silu(z):
    return z * _sigmoid(z)


def _block_id(idx, size):
    shift = size.bit_length() - 1
    assert 1 << shift == size
    return jnp.right_shift(idx, shift)


def _rmsnorm_kernel(x_ref, g_ref, o_ref):
    x = x_ref[...]
    ms = jnp.mean(x * x, axis=-1, keepdims=True)
    o_ref[...] = (x * lax.rsqrt(ms + EPS) * g_ref[...]).astype(o_ref.dtype)


def rmsnorm(x, g, out_dtype, tm=512):
    m, d = x.shape
    return pl.pallas_call(
        _rmsnorm_kernel,
        grid=(m // tm,),
        in_specs=[pl.BlockSpec((tm, d), lambda i: (i, 0)),
                  pl.BlockSpec((1, d), lambda i: (0, 0))],
        out_specs=pl.BlockSpec((tm, d), lambda i: (i, 0)),
        out_shape=jax.ShapeDtypeStruct((m, d), out_dtype),
        compiler_params=_params("parallel"),
        name="rmsnorm",
    )(x, g.reshape(1, d))


WS_VMEM_BUDGET = 44 * 1024 * 1024


def _ws_tiles(m, kdim, n, n_weight_blocks, out_bytes, has_res):
    for tm_pref, tn_pref in ((1024, 1024), (1024, 512), (512, 512), (512, 256), (256, 256), (256, 128)):
        tm, tn = _tile(m, tm_pref), _tile(n, tn_pref)
        weights = n_weight_blocks * kdim * tn * (2 * 4 + 2)
        acts = 2 * tm * kdim * 2
        outs = 2 * tm * tn * (out_bytes + (4 if has_res else 0))
        results = n_weight_blocks * tm * tn * 4
        if weights + acts + outs + results <= WS_VMEM_BUDGET:
            return tm, tn
    raise ValueError(f"no weight-stationary tiling fits VMEM for {(m, kdim, n)}")


def _matmul_kernel(*refs, has_res):
    if has_res:
        a_ref, w_ref, res_ref, o_ref, wb_ref = refs
    else:
        a_ref, w_ref, o_ref, wb_ref = refs

    @pl.when(pl.program_id(1) == 0)
    def _():
        wb_ref[...] = w_ref[...].astype(BF16)

    r = _dot(a_ref[...], wb_ref[...])
    if has_res:
        r = r + res_ref[...]
    o_ref[...] = r.astype(o_ref.dtype)


def matmul(a, w, layer, n_cols, out_dtype, res=None):
    m, kdim = a.shape
    tm, tn = _ws_tiles(m, kdim, n_cols, 1, jnp.dtype(out_dtype).itemsize, res is not None)
    in_specs = [pl.BlockSpec((tm, kdim), lambda j, i: (i, 0)),
                pl.BlockSpec((kdim, tn), lambda j, i: (layer, j))]
    args = [a, w.reshape(-1, w.shape[2])]
    if res is not None:
        in_specs.append(pl.BlockSpec((tm, tn), lambda j, i: (i, j)))
        args.append(res)
    return pl.pallas_call(
        functools.partial(_matmul_kernel, has_res=res is not None),
        grid=(n_cols // tn, m // tm),
        in_specs=in_specs,
        out_specs=pl.BlockSpec((tm, tn), lambda j, i: (i, j)),
        out_shape=jax.ShapeDtypeStruct((m, n_cols), out_dtype),
        scratch_shapes=[pltpu.VMEM((kdim, tn), BF16)],
        compiler_params=_params("parallel", "arbitrary"),
        name="matmul",
    )(*args)


def _swiglu_up_kernel(a_ref, wg_ref, wu_ref, o_ref, wgb_ref, wub_ref):
    @pl.when(pl.program_id(1) == 0)
    def _():
        wgb_ref[...] = wg_ref[...].astype(BF16)
        wub_ref[...] = wu_ref[...].astype(BF16)

    a = a_ref[...]
    o_ref[...] = (_silu(_dot(a, wgb_ref[...])) * _dot(a, wub_ref[...])).astype(o_ref.dtype)


def swiglu_up(a, w_gate_up, layer):
    m, kdim = a.shape
    f = w_gate_up.shape[2] // 2
    tm, tn = _ws_tiles(m, kdim, f, 2, 2, False)
    nj = f // tn
    return pl.pallas_call(
        _swiglu_up_kernel,
        grid=(nj, m // tm),
        in_specs=[pl.BlockSpec((tm, kdim), lambda j, i: (i, 0)),
                  pl.BlockSpec((None, kdim, tn), lambda j, i: (layer, 0, j)),
                  pl.BlockSpec((None, kdim, tn), lambda j, i: (layer, 0, j + nj))],
        out_specs=pl.BlockSpec((tm, tn), lambda j, i: (i, j)),
        out_shape=jax.ShapeDtypeStruct((m, f), BF16),
        scratch_shapes=[pltpu.VMEM((kdim, tn), BF16), pltpu.VMEM((kdim, tn), BF16)],
        compiler_params=_params("parallel", "arbitrary"),
        name="swiglu_up",
    )(a, w_gate_up, w_gate_up)


def _small_proj_kernel(h_ref, g_ref, w_ref, o_ref):
    x = h_ref[...]
    ms = jnp.mean(x * x, axis=-1, keepdims=True)
    xn = x * lax.rsqrt(ms + EPS) * g_ref[...]
    xh, xl = _split2(xn)
    wh, wl = _split2(w_ref[...])
    o_ref[...] = _dot(xh, wh) + _dot(xl, wh) + _dot(xh, wl)


def small_proj(h, g, w, tm=512):
    m, d = h.shape
    n = w.shape[1]
    return pl.pallas_call(
        _small_proj_kernel,
        grid=(m // tm,),
        in_specs=[pl.BlockSpec((tm, d), lambda i: (i, 0)),
                  pl.BlockSpec((1, d), lambda i: (0, 0)),
                  pl.BlockSpec((d, n), lambda i: (0, 0))],
        out_specs=pl.BlockSpec((tm, n), lambda i: (i, 0)),
        out_shape=jax.ShapeDtypeStruct((m, n), F32),
        compiler_params=_params("parallel"),
        name="small_proj",
    )(h, g.reshape(1, d), w)


def _pad_cols(w, width):
    return jnp.pad(w, ((0, 0), (0, width - w.shape[1])))


def _split_heads(x, n):
    return jnp.stack([x[:, h * HEAD_DIM:(h + 1) * HEAD_DIM] for h in range(n)], axis=0)


def _sb_attn_kernel(q_ref, k_ref, v_ref, o_ref, *, t, hb, scale):
    i = pl.program_id(2)
    q = _split_heads(q_ref[0], hb)
    row = lax.broadcasted_iota(jnp.int32, (t, t), 0)
    col = lax.broadcasted_iota(jnp.int32, (t, t), 1)
    strict = col < row
    suffix = jnp.where(row > col, 1.0, 0.0).astype(BF16)

    def block(kb, carry, acc, masked):
        start = pl.multiple_of(kb * t, t)
        k = _split_heads(k_ref[0, pl.ds(start, t), :], hb)
        v = _split_heads(v_ref[0, pl.ds(start, t), :], hb)
        z = _bmm_nt(q, k) * scale
        sp = _softplus(z)
        log_keep = -sp
        if masked:
            log_keep = jnp.where(strict, log_keep, 0.0)
        hi, lo = _split2(log_keep.reshape(hb * t, t))
        later = (_dot(hi, suffix) + _dot(lo, suffix)).reshape(hb, t, t) + carry
        w = jnp.exp(z - sp + later)
        if masked:
            w = jnp.where(strict, w, 0.0)
        acc = acc + _bmm(w.astype(BF16), v)
        carry = carry + jnp.sum(log_keep, axis=2, keepdims=True)
        return carry, acc

    carry, acc = block(i, jnp.zeros((hb, t, 1), F32), jnp.zeros((hb, t, HEAD_DIM), F32), True)

    def live(c):
        return (c[0] < i) & (c[3] > EXP_UNDERFLOW)

    def body(c):
        carry, acc = block(i - 1 - c[0], c[1], c[2], False)
        return c[0] + 1, carry, acc, jnp.max(carry)

    _, carry, acc, _ = lax.while_loop(live, body, (jnp.int32(0), carry, acc, jnp.max(carry)))
    for h in range(hb):
        o_ref[0, :, h * HEAD_DIM:(h + 1) * HEAD_DIM] = acc[h].astype(o_ref.dtype)


def sb_attention(qkv, n_heads, t=256, hb=4):
    b, s, _ = qkv.shape
    hb = min(hb, n_heads)
    assert n_heads % hb == 0
    g = n_heads // hb
    w = hb * HEAD_DIM
    return pl.pallas_call(
        functools.partial(_sb_attn_kernel, t=t, hb=hb, scale=HEAD_DIM ** -0.5),
        grid=(b, g, s // t),
        in_specs=[pl.BlockSpec((1, t, w), lambda bi, gi, i: (bi, i, gi)),
                  pl.BlockSpec((1, s, w), lambda bi, gi, i: (bi, 0, g + gi)),
                  pl.BlockSpec((1, s, w), lambda bi, gi, i: (bi, 0, 2 * g + gi))],
        out_specs=pl.BlockSpec((1, t, w), lambda bi, gi, i: (bi, i, gi)),
        out_shape=jax.ShapeDtypeStruct((b, s, n_heads * HEAD_DIM), BF16),
        compiler_params=_params("parallel", "parallel", "arbitrary"),
        name="sb_attention",
    )(qkv, qkv, qkv)


def _fox_gate_kernel(f_ref, bias_ref, col_ref, row_ref, carry_ref, *, t):
    i = pl.program_id(1)

    @pl.when(i == 0)
    def _():
        carry_ref[...] = jnp.zeros_like(carry_ref)

    x = f_ref[0] + bias_ref[...]
    log_f = -_softplus(-x)
    row = lax.broadcasted_iota(jnp.int32, (t, t), 0)
    col = lax.broadcasted_iota(jnp.int32, (t, t), 1)
    lower = jnp.where(col <= row, 1.0, 0.0).astype(BF16)
    hi, mid, lo = _split3(log_f)
    cum = _dot(lower, hi) + _dot(lower, mid) + _dot(lower, lo) + carry_ref[...]
    col_ref[0] = cum
    row_ref[0] = cum.T
    carry_ref[...] = cum[t - 1:t, :]


def fox_gate(f_raw, bias, t=512):
    b, s, n = f_raw.shape
    return pl.pallas_call(
        functools.partial(_fox_gate_kernel, t=t),
        grid=(b, s // t),
        in_specs=[pl.BlockSpec((1, t, n), lambda bi, i: (bi, i, 0)),
                  pl.BlockSpec((1, n), lambda bi, i: (0, 0))],
        out_specs=[pl.BlockSpec((1, t, n), lambda bi, i: (bi, i, 0)),
                   pl.BlockSpec((1, n, t), lambda bi, i: (bi, 0, i))],
        out_shape=[jax.ShapeDtypeStruct((b, s, n), F32), jax.ShapeDtypeStruct((b, n, s), F32)],
        scratch_shapes=[pltpu.VMEM((1, n), F32)],
        compiler_params=_params("parallel", "arbitrary"),
        name="fox_gate",
    )(f_raw, bias)


def _fox_attn_kernel(q_ref, k_ref, v_ref, gate_ref, cq_ref, ck_ref, o_ref, *, t, hb, scale):
    gi = pl.program_id(1)
    i = pl.program_id(2)
    q = _split_heads(q_ref[0], hb)
    lane = lax.broadcasted_iota(jnp.int32, (t, LANES), 1)
    cq = cq_ref[0]
    cum_q = jnp.stack([jnp.sum(jnp.where(lane == gi * hb + h, cq, 0.0), axis=1, keepdims=True)
                       for h in range(hb)], axis=0) * LOG2E
    row = lax.broadcasted_iota(jnp.int32, (t, t), 0)
    col = lax.broadcasted_iota(jnp.int32, (t, t), 1)
    causal = col <= row

    def block(kb, m, l, acc, masked):
        start = pl.multiple_of(kb * t, t)
        k = _split_heads(k_ref[0, pl.ds(start, t), :], hb)
        v = _split_heads(v_ref[0, pl.ds(start, t), :], hb)
        cum_k = ck_ref[0, :, pl.ds(kb, 1), :] * LOG2E
        logits = _bmm_nt(q, k) * (scale * LOG2E) + cum_q - cum_k
        if masked:
            logits = jnp.where(causal, logits, -jnp.inf)
        m_new = jnp.maximum(m, jnp.max(logits, axis=2, keepdims=True))
        alpha = jnp.exp2(m - m_new)
        p = jnp.exp2(logits - m_new)
        l = l * alpha + jnp.sum(p, axis=2, keepdims=True)
        acc = acc * alpha + _bmm(p.astype(BF16), v)
        return m_new, l, acc

    init = (jnp.full((hb, t, 1), -jnp.inf, F32), jnp.zeros((hb, t, 1), F32), jnp.zeros((hb, t, HEAD_DIM), F32))
    m, l, acc = block(i, *init, True)

    def body(n, c):
        return block(i - 1 - n, *c, False)

    m, l, acc = lax.fori_loop(0, i, body, (m, l, acc))
    out = acc / l
    for h in range(hb):
        hs = slice(h * HEAD_DIM, (h + 1) * HEAD_DIM)
        o_ref[0, :, hs] = (out[h] * _sigmoid(gate_ref[0, :, hs].astype(F32))).astype(o_ref.dtype)


def fox_attention(proj, cum_col, cum_row, n_heads, t=512, hb=4):
    b, s, _ = proj.shape
    hb = min(hb, n_heads)
    assert n_heads % hb == 0
    g = n_heads // hb
    w = hb * HEAD_DIM
    cum_row = cum_row.reshape(b, n_heads, s // t, t)
    return pl.pallas_call(
        functools.partial(_fox_attn_kernel, t=t, hb=hb, scale=HEAD_DIM ** -0.5),
        grid=(b, g, s // t),
        in_specs=[pl.BlockSpec((1, t, w), lambda bi, gi, i: (bi, i, gi)),
                  pl.BlockSpec((1, s, w), lambda bi, gi, i: (bi, 0, g + gi)),
                  pl.BlockSpec((1, s, w), lambda bi, gi, i: (bi, 0, 2 * g + gi)),
                  pl.BlockSpec((1, t, w), lambda bi, gi, i: (bi, i, 3 * g + gi)),
                  pl.BlockSpec((1, t, LANES), lambda bi, gi, i: (bi, i, 0)),
                  pl.BlockSpec((1, hb, s // t, t), lambda bi, gi, i: (bi, gi, 0, 0))],
        out_specs=pl.BlockSpec((1, t, w), lambda bi, gi, i: (bi, i, gi)),
        out_shape=jax.ShapeDtypeStruct((b, s, n_heads * HEAD_DIM), BF16),
        compiler_params=_params("parallel", "parallel", "arbitrary"),
        name="fox_attention",
    )(proj, proj, proj, proj, cum_col, cum_row)


def _gdn_gate_kernel(b_ref, a_ref, alog_ref, dt_ref, beta_ref, gc_ref, gl_ref, gct_ref, *, t, chunk):
    beta_ref[0] = _sigmoid(b_ref[0])
    g = -jnp.exp(alog_ref[...]) * _softplus(a_ref[0] + dt_ref[...])
    row = lax.broadcasted_iota(jnp.int32, (t, t), 0)
    col = lax.broadcasted_iota(jnp.int32, (t, t), 1)
    same = _block_id(row, chunk) == _block_id(col, chunk)
    lower = jnp.where(same & (col <= row), 1.0, 0.0).astype(BF16)
    whole = jnp.where(same, 1.0, 0.0).astype(BF16)
    hi, mid, lo = _split3(g)
    gc = _dot(lower, hi) + _dot(lower, mid) + _dot(lower, lo)
    gc_ref[0] = gc
    gl_ref[0] = _dot(whole, hi) + _dot(whole, mid) + _dot(whole, lo)
    gct_ref[0] = gc.T


def gdn_gates(b_raw, a_raw, a_log, dt_bias, t=512):
    b, s, n = b_raw.shape
    spec = pl.BlockSpec((1, t, n), lambda bi, i: (bi, i, 0))
    vec = pl.BlockSpec((1, n), lambda bi, i: (0, 0))
    return pl.pallas_call(
        functools.partial(_gdn_gate_kernel, t=t, chunk=GDN_CHUNK),
        grid=(b, s // t),
        in_specs=[spec, spec, vec, vec],
        out_specs=[spec, spec, spec, pl.BlockSpec((1, n, t), lambda bi, i: (bi, 0, i))],
        out_shape=[jax.ShapeDtypeStruct((b, s, n), F32)] * 3 + [jax.ShapeDtypeStruct((b, n, s), F32)],
        compiler_params=_params("parallel", "parallel"),
        name="gdn_gates",
    )(b_raw, a_raw, a_log, dt_bias)


def _gdn_prep_kernel(x_ref, prev_ref, w_ref, o_ref, *, ts, n_q_blocks, n_qk_blocks, heads_per_block):
    si = pl.program_id(1)
    ci = pl.program_id(2)
    cur = x_ref[0].astype(F32)
    prev = jnp.where(si == 0, 0.0, prev_ref[0].astype(F32))
    ext = jnp.concatenate([prev, cur], axis=0)
    w = w_ref[...]
    y = jnp.zeros_like(cur)
    for tap in range(GDN_CONV):
        off = 8 - (GDN_CONV - 1) + tap
        y = y + w[tap:tap + 1, :] * ext[off:off + ts, :]
    y = _silu(y)
    q_scale = jnp.where(ci < n_q_blocks, HEAD_DIM ** -0.5, 1.0)
    is_qk = ci < n_qk_blocks
    for hh in range(heads_per_block):
        sl = slice(hh * HEAD_DIM, (hh + 1) * HEAD_DIM)
        yh = y[:, sl]
        ss = jnp.sum(yh * yh, axis=1, keepdims=True)
        factor = jnp.where(is_qk, lax.rsqrt(ss + EPS) * q_scale, 1.0)
        o_ref[0, :, sl] = (yh * factor).astype(o_ref.dtype)


def gdn_prep(proj, conv_w_t, qkv_dim, k_dim, ts=512, tc=512):
    b, s, _ = proj.shape
    ts, tc = min(ts, s), min(tc, k_dim)
    rows8 = ts // 8
    return pl.pallas_call(
        functools.partial(_gdn_prep_kernel, ts=ts, n_q_blocks=k_dim // tc, n_qk_blocks=2 * k_dim // tc,
                          heads_per_block=tc // HEAD_DIM),
        grid=(b, s // ts, qkv_dim // tc),
        in_specs=[pl.BlockSpec((1, ts, tc), lambda bi, si, ci: (bi, si, ci)),
                  pl.BlockSpec((1, 8, tc), lambda bi, si, ci: (bi, jnp.maximum(si * rows8 - 1, 0), ci)),
                  pl.BlockSpec((GDN_CONV, tc), lambda bi, si, ci: (0, ci))],
        out_specs=pl.BlockSpec((1, ts, tc), lambda bi, si, ci: (bi, si, ci)),
        out_shape=jax.ShapeDtypeStruct((b, s, qkv_dim), BF16),
        compiler_params=_params("parallel", "parallel", "parallel"),
        name="gdn_prep",
    )(proj, proj, conv_w_t)


def _bmm(a, b):
    return lax.dot_general(a, b, (((2,), (1,)), ((0,), (0,))), preferred_element_type=F32)


def _bmm_nt(a, b):
    return lax.dot_general(a, b, (((2,), (2,)), ((0,), (0,))), preferred_element_type=F32)


def _bmm_tn(a, b):
    return lax.dot_general(a, b, (((1,), (1,)), ((0,), (0,))), preferred_element_type=F32)


def _inv_unit_lower(m_strict, row, col):
    c = m_strict.shape[-1]
    eye = jnp.where(row == col, 1.0, 0.0)
    n1 = jnp.where(_block_id(row, 8) == _block_id(col, 8), -m_strict, 0.0)
    n1b = n1.astype(BF16)
    n2 = _bmm(n1b, n1b)
    n2b = n2.astype(BF16)
    n4 = _bmm(n2b, n2b)
    x = _bmm((eye + n1).astype(BF16), (eye + n2).astype(BF16))
    x = _bmm(x.astype(BF16), (eye + n4).astype(BF16))
    size = 8
    while size < c:
        lower_left = ((_block_id(row, size) == _block_id(col, size) + 1)
                      & (_block_id(row, 2 * size) == _block_id(col, 2 * size)))
        cm = jnp.where(lower_left, m_strict, 0.0).astype(BF16)
        xb = x.astype(BF16)
        x = x - _bmm(_bmm(xb, cm).astype(BF16), xb)
        size *= 2
    return x


def _gdn_kernel(q_ref, k_ref, v_ref, z_ref, beta_ref, gc_ref, gl_ref, gct_ref, nw_ref, o_ref, state_ref,
                *, n_chunks, nkh):
    gi = pl.program_id(1)
    si = pl.program_id(2)
    c = GDN_CHUNK
    t = n_chunks * c
    nvh = 2 * nkh

    @pl.when(si == 0)
    def _():
        state_ref[...] = jnp.zeros_like(state_ref)

    lane = lax.broadcasted_iota(jnp.int32, (t, LANES), 1)
    row = lax.broadcasted_iota(jnp.int32, (c, c), 0)
    col = lax.broadcasted_iota(jnp.int32, (c, c), 1)
    strict = col < row
    incl = col <= row

    nc = n_chunks

    def column(ref, head):
        col_vec = jnp.sum(jnp.where(lane == head, ref[0], 0.0), axis=1, keepdims=True)
        return col_vec.reshape(nc, c, 1)

    def heads(per_head):
        return jnp.concatenate([per_head(vh) for vh in range(nvh)], axis=0)

    def head_cols(ref, vh):
        return ref[0, :, vh * HEAD_DIM:(vh + 1) * HEAD_DIM].reshape(nc, c, HEAD_DIM)

    beta = heads(lambda vh: column(beta_ref, gi * nvh + vh))
    gcum = heads(lambda vh: column(gc_ref, gi * nvh + vh))
    glast = heads(lambda vh: column(gl_ref, gi * nvh + vh))
    g_row = heads(lambda vh: gct_ref[0, vh // 2, :, vh % 2:vh % 2 + 1, :])
    q = [head_cols(q_ref, kh) for kh in range(nkh)]
    k = [head_cols(k_ref, kh) for kh in range(nkh)]
    kk = [_bmm_nt(k[kh], k[kh]) for kh in range(nkh)]
    qk = [_bmm_nt(q[kh], k[kh]) for kh in range(nkh)]
    qf = heads(lambda vh: q[vh // 2].astype(F32))
    kf = heads(lambda vh: k[vh // 2].astype(F32))
    v = heads(lambda vh: head_cols(v_ref, vh)).astype(F32)

    decay = jnp.exp(jnp.minimum(gcum - g_row, 0.0))
    m_strict = jnp.where(strict, heads(lambda vh: kk[vh // 2]) * beta * decay, 0.0)
    attn = jnp.where(incl, heads(lambda vh: qk[vh // 2]) * decay, 0.0).astype(BF16)
    t_inv = _inv_unit_lower(m_strict, row, col).astype(BF16)
    u = _bmm(t_inv, (v * beta).astype(BF16))
    w = _bmm(t_inv, (kf * (beta * jnp.exp(gcum))).astype(BF16)).astype(BF16)
    q_dec = (qf * jnp.exp(gcum)).astype(BF16)
    k_dec = (kf * jnp.exp(glast - gcum)).astype(BF16)
    chunk_decay = jnp.exp(glast[:, 0:1, :])
    nw = nw_ref[...]

    state = state_ref[...]
    for ch in range(nc):
        def pick(x):
            return jnp.concatenate([x[vh * nc + ch:vh * nc + ch + 1] for vh in range(nvh)], axis=0)

        state_b = state.astype(BF16)
        v_new = pick(u) - _bmm(pick(w), state_b)
        v_new_b = v_new.astype(BF16)
        o = _bmm(pick(q_dec), state_b) + _bmm(pick(attn), v_new_b)
        state = state * pick(chunk_decay) + _bmm_tn(pick(k_dec), v_new_b)
        ms = jnp.mean(o * o, axis=2, keepdims=True)
        rows = slice(ch * c, (ch + 1) * c)
        zf = z_ref[0, rows, :].astype(F32)
        for vh in range(nvh):
            hs = slice(vh * HEAD_DIM, (vh + 1) * HEAD_DIM)
            o_ref[0, rows, hs] = (o[vh] * lax.rsqrt(ms[vh] + EPS) * nw * _silu(zf[:, hs])).astype(o_ref.dtype)
    state_ref[...] = state


def gdn_core(qkv_act, proj, beta, gc, gl, gct, norm_w, n_k_heads, n_chunks=4, nkh=2):
    b, s, _ = qkv_act.shape
    nkh = min(nkh, n_k_heads)
    assert n_k_heads % nkh == 0
    g = n_k_heads // nkh
    kw, vw = nkh * HEAD_DIM, 2 * nkh * HEAD_DIM
    n_chunks = min(n_chunks, s // GDN_CHUNK)
    t = n_chunks * GDN_CHUNK
    gate = pl.BlockSpec((1, t, LANES), lambda bi, gi, si: (bi, si, 0))
    return pl.pallas_call(
        functools.partial(_gdn_kernel, n_chunks=n_chunks, nkh=nkh),
        grid=(b, g, s // t),
        in_specs=[pl.BlockSpec((1, t, kw), lambda bi, gi, si: (bi, si, gi)),
                  pl.BlockSpec((1, t, kw), lambda bi, gi, si: (bi, si, g + gi)),
                  pl.BlockSpec((1, t, vw), lambda bi, gi, si: (bi, si, g + gi)),
                  pl.BlockSpec((1, t, vw), lambda bi, gi, si: (bi, si, 2 * g + gi)),
                  gate, gate, gate,
                  pl.BlockSpec((1, nkh, n_chunks, 2, GDN_CHUNK), lambda bi, gi, si: (bi, gi, si, 0, 0)),
                  pl.BlockSpec((1, HEAD_DIM), lambda bi, gi, si: (0, 0))],
        out_specs=pl.BlockSpec((1, t, vw), lambda bi, gi, si: (bi, si, gi)),
        out_shape=jax.ShapeDtypeStruct((b, s, 2 * n_k_heads * HEAD_DIM), BF16),
        scratch_shapes=[pltpu.VMEM((2 * nkh, HEAD_DIM, HEAD_DIM), F32)],
        compiler_params=_params("parallel", "parallel", "arbitrary"),
        name="gdn_core",
    )(qkv_act, qkv_act, qkv_act, proj, beta, gc, gl, gct, norm_w)


def _router_kernel(logit_ref, idx_ref, w_ref, *, n_experts):
    x = logit_ref[...]
    lane = lax.broadcasted_iota(jnp.int32, x.shape, 1)
    x = jnp.where(lane < n_experts, x, -jnp.inf)
    m1 = jnp.max(x, axis=1, keepdims=True)
    i1 = jnp.min(jnp.where(x == m1, lane, LANES), axis=1, keepdims=True)
    x2 = jnp.where(lane == i1, -jnp.inf, x)
    m2 = jnp.max(x2, axis=1, keepdims=True)
    i2 = jnp.min(jnp.where(x2 == m2, lane, LANES), axis=1, keepdims=True)
    e2 = jnp.exp(m2 - m1)
    w1 = 1.0 / (1.0 + e2)
    w2 = e2 / (1.0 + e2)
    idx_ref[...] = jnp.where(lane == 0, i1, jnp.where(lane == 1, i2, 0))
    w_ref[...] = jnp.where(lane == 0, w1, jnp.where(lane == 1, w2, 0.0))


def router_top2(logits, n_experts, tm=1024):
    m, n = logits.shape
    spec = pl.BlockSpec((tm, n), lambda i: (i, 0))
    return pl.pallas_call(
        functools.partial(_router_kernel, n_experts=n_experts),
        grid=(m // tm,),
        in_specs=[spec],
        out_specs=[spec, spec],
        out_shape=[jax.ShapeDtypeStruct((m, n), jnp.int32), jax.ShapeDtypeStruct((m, n), F32)],
        compiler_params=_params("parallel"),
        name="router",
    )(logits)


MOE_ROW_TILE = 512
MOE_UP_COLS = 1408
MOE_DOWN_COLS = 1024


def _route(idx, n_experts, tm):
    m = idx.shape[0]
    a = 2 * m
    e = idx.reshape(a)
    onehot = (e[:, None] == jnp.arange(n_experts, dtype=jnp.int32)[None, :]).astype(jnp.int32)
    csum = jnp.cumsum(onehot, axis=0)
    counts = csum[-1]
    padded = (counts + tm - 1) // tm * tm
    ends = jnp.cumsum(padded)
    pos = jnp.sum(onehot * (csum - 1 + (ends - padded)[None, :]), axis=1)
    n_tiles = a // tm + n_experts
    tile_start = jnp.arange(n_tiles, dtype=jnp.int32) * tm
    tile_expert = jnp.minimum(jnp.sum((tile_start[:, None] >= ends[None, :]).astype(jnp.int32), axis=1),
                              n_experts - 1)
    n_used = (ends[-1] // tm).astype(jnp.int32).reshape(1)
    row_token = jnp.zeros((n_tiles * tm,), jnp.int32).at[pos].set(jnp.arange(a, dtype=jnp.int32) // 2)
    return pos.astype(jnp.int32), tile_expert, n_used, row_token


DMA_ISSUE_UNROLL = 8


def _row_copy(src_hbm, src_row, dst_vmem, dst_row, sem):
    return pltpu.make_async_copy(src_hbm.at[pl.ds(src_row, 1)], dst_vmem.at[pl.ds(dst_row, 1)], sem)


def _rows_wait(src_hbm, dst_vmem, sem):
    pltpu.make_async_copy(src_hbm.at[pl.ds(0, dst_vmem.shape[0])], dst_vmem, sem).wait()


def _moe_gather_kernel(row_token_ref, n_used_ref, h_hbm, g_ref, o_ref, buf, sem, *, tm):
    i = pl.program_id(0)
    n_used = n_used_ref[0]

    def issue(tile):
        slot = lax.rem(tile, 2)

        def start(group, c):
            for u in range(DMA_ISSUE_UNROLL):
                r = group * DMA_ISSUE_UNROLL + u
                _row_copy(h_hbm, row_token_ref[tile * tm + r], buf.at[slot], r, sem.at[slot]).start(priority=u % 2)
            return c

        lax.fori_loop(0, tm // DMA_ISSUE_UNROLL, start, 0)

    @pl.when(i == 0)
    def _():
        issue(i)

    @pl.when(i + 1 < n_used)
    def _():
        issue(i + 1)

    @pl.when(i < n_used)
    def _():
        slot = lax.rem(i, 2)
        _rows_wait(h_hbm, buf.at[slot], sem.at[slot])
        x = buf[slot]
        ms = jnp.mean(x * x, axis=-1, keepdims=True)
        o_ref[...] = (x * lax.rsqrt(ms + EPS) * g_ref[...]).astype(o_ref.dtype)

    @pl.when(i >= n_used)
    def _():
        o_ref[...] = jnp.zeros_like(o_ref)


def moe_gather(h, g, row_token, n_used, tm):
    m, d = h.shape
    p = row_token.shape[0]
    return pl.pallas_call(
        functools.partial(_moe_gather_kernel, tm=tm),
        grid_spec=pltpu.PrefetchScalarGridSpec(
            num_scalar_prefetch=2,
            grid=(p // tm,),
            in_specs=[pl.BlockSpec(memory_space=pl.ANY),
                      pl.BlockSpec((1, d), lambda i, rt, nu: (0, 0))],
            out_specs=pl.BlockSpec((tm, d), lambda i, rt, nu: (i, 0)),
            scratch_shapes=[pltpu.VMEM((2, tm, d), F32), pltpu.SemaphoreType.DMA((2,))]),
        out_shape=jax.ShapeDtypeStruct((p, d), BF16),
        compiler_params=_params("arbitrary"),
        name="moe_gather",
    )(row_token, n_used, h, g.reshape(1, d))


def _moe_matmul_kernel(te_ref, n_used_ref, x_ref, w_ref, *refs, gated):
    if gated:
        gate_ref, o_ref, wb_ref = refs
    else:
        o_ref, wb_ref = refs
    i = pl.program_id(1)
    n_used = n_used_ref[0]

    @pl.when((i == 0) | (te_ref[i] != te_ref[jnp.maximum(i - 1, 0)]))
    def _():
        wb_ref[...] = w_ref[...].astype(BF16)

    @pl.when(i < n_used)
    def _():
        r = _dot(x_ref[...], wb_ref[...])
        if gated:
            r = _silu(gate_ref[...].astype(F32)) * r
        o_ref[...] = r.astype(o_ref.dtype)

    @pl.when(i >= n_used)
    def _():
        o_ref[...] = jnp.zeros_like(o_ref)


def moe_matmul(xs, w, layer, col_block, n_cols, tn, out_dtype, tile_expert, n_used, tm, gate=None, name="moe_matmul"):
    p, kdim = xs.shape
    nj = n_cols // tn

    def row_tile(j, i, te, nu):
        return (jnp.minimum(i, nu[0] - 1), 0)

    in_specs = [pl.BlockSpec((tm, kdim), row_tile),
                pl.BlockSpec((None, None, kdim, tn), lambda j, i, te, nu: (layer, te[i], 0, j + col_block))]
    args = [tile_expert, n_used, xs, w]
    if gate is not None:
        in_specs.append(pl.BlockSpec((tm, tn), lambda j, i, te, nu: (jnp.minimum(i, nu[0] - 1), j)))
        args.append(gate)
    return pl.pallas_call(
        functools.partial(_moe_matmul_kernel, gated=gate is not None),
        grid_spec=pltpu.PrefetchScalarGridSpec(
            num_scalar_prefetch=2,
            grid=(nj, p // tm),
            in_specs=in_specs,
            out_specs=pl.BlockSpec((tm, tn), lambda j, i, te, nu: (i, j)),
            scratch_shapes=[pltpu.VMEM((kdim, tn), BF16)]),
        out_shape=jax.ShapeDtypeStruct((p, n_cols), out_dtype),
        compiler_params=_params("parallel", "arbitrary"),
        name=name,
    )(*args)


def _moe_combine_kernel(pos_ref, h_ref, w_ref, g_ref, y_hbm, *refs, tm, emit_h):
    if emit_h:
        o_ref, n_ref, buf, sem = refs
    else:
        n_ref, buf, sem = refs
    i = pl.program_id(0)

    def issue(tile):
        slot = lax.rem(tile, 2)

        def start(group, c):
            for u in range(DMA_ISSUE_UNROLL):
                r = group * DMA_ISSUE_UNROLL + u
                for k in range(2):
                    _row_copy(y_hbm, pos_ref[2 * (tile * tm + r) + k], buf.at[slot, k], r,
                              sem.at[slot]).start(priority=k)
            return c

        lax.fori_loop(0, tm // DMA_ISSUE_UNROLL, start, 0)

    @pl.when(i == 0)
    def _():
        issue(i)

    @pl.when(i + 1 < pl.num_programs(0))
    def _():
        issue(i + 1)

    slot = lax.rem(i, 2)
    for k in range(2):
        _rows_wait(y_hbm, buf.at[slot, k], sem.at[slot])
    w = w_ref[...]
    x = h_ref[...] + w[:, 0:1] * buf[slot, 0] + w[:, 1:2] * buf[slot, 1]
    if emit_h:
        o_ref[...] = x
    ms = jnp.mean(x * x, axis=-1, keepdims=True)
    n_ref[...] = (x * lax.rsqrt(ms + EPS) * g_ref[...]).astype(n_ref.dtype)


def moe_combine(h, top_w, ys, pos, g_next, norm_dtype, emit_h, tm=256):
    m, d = h.shape
    row_block = pl.BlockSpec((tm, d), lambda i, pos: (i, 0))
    out_specs = [row_block, row_block] if emit_h else [row_block]
    out_shape = [jax.ShapeDtypeStruct((m, d), F32)] if emit_h else []
    out_shape.append(jax.ShapeDtypeStruct((m, d), norm_dtype))
    outs = pl.pallas_call(
        functools.partial(_moe_combine_kernel, tm=tm, emit_h=emit_h),
        grid_spec=pltpu.PrefetchScalarGridSpec(
            num_scalar_prefetch=1,
            grid=(m // tm,),
            in_specs=[row_block,
                      pl.BlockSpec((tm, LANES), lambda i, pos: (i, 0)),
                      pl.BlockSpec((1, d), lambda i, pos: (0, 0)),
                      pl.BlockSpec(memory_space=pl.ANY)],
            out_specs=out_specs,
            scratch_shapes=[pltpu.VMEM((2, 2, tm, d), F32), pltpu.SemaphoreType.DMA((2,))]),
        out_shape=out_shape,
        compiler_params=_params("arbitrary"),
        name="moe_combine",
    )(pos, h, top_w, g_next.reshape(1, d), ys)
    return (outs[0], outs[1]) if emit_h else (None, outs[0])


def _sb_layer(h, hn, w_in, w_out, layer):
    b, s, d = h.shape
    n_heads = w_out.shape[1] // HEAD_DIM
    qkv = matmul(hn, w_in, layer, w_in.shape[2], BF16)
    o = sb_attention(qkv.reshape(b, s, -1), n_heads)
    return matmul(o.reshape(b * s, -1), w_out, layer, d, F32, res=h.reshape(b * s, d)).reshape(b, s, d)


def _fox_layer(h, hn, g, w_in, b_f, w_out, layer):
    b, s, d = h.shape
    n_heads = w_out.shape[1] // HEAD_DIM
    main = 4 * n_heads * HEAD_DIM
    h2 = h.reshape(b * s, d)
    proj = matmul(hn, w_in, layer, main, BF16).reshape(b, s, main)
    f_raw = small_proj(h2, g, _pad_cols(w_in[layer, :, main:], LANES)).reshape(b, s, LANES)
    bias = jnp.pad(b_f, (0, LANES - n_heads)).reshape(1, LANES)
    cum_col, cum_t = fox_gate(f_raw, bias)
    o = fox_attention(proj, cum_col, cum_t[:, :n_heads, :], n_heads)
    return matmul(o.reshape(b * s, -1), w_out, layer, d, F32, res=h2).reshape(b, s, d)


def _gdn_layer(h, hn, g, w_in, conv_w, a_log, dt_bias, norm_w, w_out, layer):
    b, s, d = h.shape
    v_dim = w_out.shape[1]
    n_v_heads = v_dim // HEAD_DIM
    n_k_heads = n_v_heads // 2
    k_dim = n_k_heads * HEAD_DIM
    qkv_dim = 2 * k_dim + v_dim
    main = qkv_dim + v_dim
    h2 = h.reshape(b * s, d)
    proj = matmul(hn, w_in, layer, main, BF16).reshape(b, s, main)
    w_gates = jnp.concatenate([_pad_cols(w_in[layer, :, main:main + n_v_heads], LANES),
                               _pad_cols(w_in[layer, :, main + n_v_heads:], LANES)], axis=1)
    raw = small_proj(h2, g, w_gates).reshape(b, s, 2 * LANES)
    pad = (0, LANES - n_v_heads)
    beta, gc, gl, gct = gdn_gates(raw[..., :LANES], raw[..., LANES:],
                                  jnp.pad(a_log, pad).reshape(1, LANES), jnp.pad(dt_bias, pad).reshape(1, LANES))
    gct = gct[:, :n_v_heads, :].reshape(b, n_k_heads, 2, s // GDN_CHUNK, GDN_CHUNK).transpose(0, 1, 3, 2, 4)
    qkv_act = gdn_prep(proj, conv_w.T, qkv_dim, k_dim)
    o = gdn_core(qkv_act, proj, beta, gc, gl, gct, norm_w.reshape(1, HEAD_DIM), n_k_heads)
    return matmul(o.reshape(b * s, v_dim), w_out, layer, d, F32, res=h2).reshape(b, s, d)


def _dense_ffn(h, g, w_gate_up, w_down, layer):
    b, s, d = h.shape
    h2 = h.reshape(b * s, d)
    hn = rmsnorm(h2, g, BF16)
    mid = swiglu_up(hn, w_gate_up, layer)
    return matmul(mid, w_down, layer, d, F32, res=h2).reshape(b, s, d)


def _moe_ffn(h, g, w_router, w_gate_up, w_down, layer, g_next, norm_dtype, emit_h):
    b, s, d = h.shape
    n_experts = w_router.shape[1]
    f = w_down.shape[2]
    h2 = h.reshape(b * s, d)
    logits = small_proj(h2, g, _pad_cols(w_router, LANES))
    top_idx, top_w = router_top2(logits, n_experts)
    tm = MOE_ROW_TILE
    pos, tile_expert, n_used, row_token = _route(top_idx[:, :2], n_experts, tm)
    xs = moe_gather(h2, g, row_token, n_used, tm)
    tn_up = _tile(f, MOE_UP_COLS)
    gate = moe_matmul(xs, w_gate_up, layer, 0, f, tn_up, BF16, tile_expert, n_used, tm, name="moe_gate")
    mid = moe_matmul(xs, w_gate_up, layer, f // tn_up, f, tn_up, BF16, tile_expert, n_used, tm, gate=gate,
                     name="moe_up")
    ys = moe_matmul(mid, w_down, layer, 0, d, _tile(d, MOE_DOWN_COLS), F32, tile_expert, n_used, tm,
                    name="moe_down")
    new_h, normed = moe_combine(h2, top_w, ys, pos, g_next, norm_dtype, emit_h)
    return (new_h.reshape(b, s, d) if emit_h else None), normed


def kernel(x, norm_mix, norm_ffn, sb_w_in, sb_w_out, gdn_w_in, gdn_conv_w, gdn_a_log, gdn_dt_bias, gdn_norm_w,
           gdn_w_out, fox_w_in, fox_b_f, fox_w_out, ffn_w_gate_up, ffn_w_down, moe_w_router, moe_w_gate_up,
           moe_w_down, final_norm):
    depth = norm_mix.shape[0]
    b, s, d = x.shape
    h = x
    hn = None
    for i in range(depth):
        kind, j = i % 3, i // 3
        if hn is None:
            hn = rmsnorm(h.reshape(b * s, d), norm_mix[i], BF16)
        if kind == 0:
            h = _sb_layer(h, hn, sb_w_in, sb_w_out, j)
        elif kind == 1:
            h = _gdn_layer(h, hn, norm_mix[i], gdn_w_in, gdn_conv_w[j], gdn_a_log[j], gdn_dt_bias[j],
                           gdn_norm_w[j], gdn_w_out, j)
        else:
            h = _fox_layer(h, hn, norm_mix[i], fox_w_in, fox_b_f[j], fox_w_out, j)
        hn = None
        f = i // 2
        last = i == depth - 1
        if i % 2 == 0:
            h = _dense_ffn(h, norm_ffn[i], ffn_w_gate_up, ffn_w_down, f)
        elif last:
            _, out = _moe_ffn(h, norm_ffn[i], moe_w_router[f], moe_w_gate_up, moe_w_down, f,
                              final_norm, x.dtype, emit_h=False)
            return out.reshape(b, s, d)
        else:
            h, hn = _moe_ffn(h, norm_ffn[i], moe_w_router[f], moe_w_gate_up, moe_w_down, f,
                             norm_mix[i + 1], BF16, emit_h=True)
    return rmsnorm(h.reshape(b * s, d), final_norm, x.dtype).reshape(b, s, d)
```
